```python
import math
import jax
import jax.numpy as jnp
from jax import lax
import numpy as np

D_MODEL = 1024
BATCH = 16
SEQ = 2048
DEPTH = 4

GRID_W = 64
CTX_LEN = 256
N_MIXERS = 3
N_MOD = 6
EPS = 1e-6

SSM_EXPAND = 2
D_INNER = SSM_EXPAND * D_MODEL
SSM_HEAD_DIM = 64
SSM_HEADS = D_INNER // SSM_HEAD_DIM
SSM_GROUPS = 8
SSM_STATE = 128
SSM_CONV = 4
SSM_CHUNK = 128
SSM_BC = SSM_GROUPS * SSM_STATE
SSM_CONV_CH = D_INNER + 2 * SSM_BC
SSM_IN = D_INNER + SSM_CONV_CH + 2 * SSM_HEADS

DA_HEAD_DIM = 64
DA_HEADS = D_MODEL // (2 * DA_HEAD_DIM)
DA_Q_BLOCK = 128
ROPE_THETA = 10000.0

CONF_KERNEL = 31

FFN_HIDDEN = 2816
FFN_CONV = 3

N_A = len(range(0, DEPTH, N_MIXERS))
N_B = len(range(1, DEPTH, N_MIXERS))
N_C = len(range(2, DEPTH, N_MIXERS))

kernel_name = 'hybrid_ssd_diffattn_conformer_trunk'


def rms_norm(x, g):
    xf = x.astype(jnp.float32)
    y = xf * lax.rsqrt(jnp.mean(xf * xf, axis=-1, keepdims=True) + EPS)
    return (y * g.astype(jnp.float32)).astype(x.dtype)


def layer_norm(x, g, b):
    xf = x.astype(jnp.float32)
    mu = jnp.mean(xf, axis=-1, keepdims=True)
    xc = xf - mu
    y = xc * lax.rsqrt(jnp.mean(xc * xc, axis=-1, keepdims=True) + EPS)
    return (y * g.astype(jnp.float32) + b.astype(jnp.float32)).astype(x.dtype)


def modulate(x, g, shift, scale):
    return rms_norm(x, g) * (1 + scale) + shift


def dwconv(x, w, b, pad_l, pad_r):
    y = lax.conv_general_dilated(x, w[:, None, :].astype(x.dtype), window_strides=(1,),
                                 padding=[(pad_l, pad_r)],
                                 dimension_numbers=('NWC', 'WIO', 'NWC'),
                                 feature_group_count=x.shape[-1])
    return y + b.astype(x.dtype)


def flip_seq(t, rev):
    return jnp.flip(t, axis=1) if rev else t


def ssd_chunked(x, dt, a, bm, cm, h0):
    b, s, _, p = x.shape
    g, n = bm.shape[2], bm.shape[3]
    j = SSM_HEADS // g
    nc = s // SSM_CHUNK
    xr = (x * dt[..., None]).reshape(b, nc, SSM_CHUNK, g, j, p)
    la = (dt * a).reshape(b, nc, SSM_CHUNK, g, j)
    ac = jnp.moveaxis(jnp.cumsum(la, axis=2), 2, -1)
    br = bm.reshape(b, nc, SSM_CHUNK, g, n)
    cr = cm.reshape(b, nc, SSM_CHUNK, g, n)
    mask = jnp.tril(jnp.ones((SSM_CHUNK, SSM_CHUNK), dtype=bool))
    seg = ac[..., :, None] - ac[..., None, :]
    lmat = jnp.exp(jnp.where(mask, seg, -jnp.inf))
    cb = jnp.einsum('bclgn,bcsgn->bcgls', cr, br)
    y_diag = jnp.einsum('bcgls,bcgjls,bcsgjp->bclgjp', cb, lmat, xr)
    decay_to_end = jnp.exp(ac[..., -1:] - ac)
    states = jnp.einsum('bclgn,bcgjl,bclgjp->bcgjpn', br, decay_to_end, xr)
    chunk_decay = jnp.exp(ac[..., -1])

    def step(h, inp):
        st, dec = inp
        return h * dec[..., None, None] + st, h

    h_final, h_prev = lax.scan(step, h0.reshape(b, g, j, p, n),
                               (jnp.moveaxis(states, 1, 0), jnp.moveaxis(chunk_decay, 1, 0)))
    h_prev = jnp.moveaxis(h_prev, 0, 1)
    y_off = jnp.einsum('bclgn,bcgjpn,bcgjl->bclgjp', cr, h_prev, jnp.exp(ac))
    y = (y_diag + y_off).reshape(b, s, SSM_HEADS, p)
    return y, h_final.reshape(b, SSM_HEADS, p, n)


def ssm_prep(xbc, dt_raw, conv_w, conv_b, dt_bias):
    u = jax.nn.silu(dwconv(xbc, conv_w, conv_b, SSM_CONV - 1, 0))
    b, n = u.shape[0], u.shape[1]
    xs = u[..., :D_INNER].reshape(b, n, SSM_HEADS, SSM_HEAD_DIM)
    bm = u[..., D_INNER:D_INNER + SSM_BC].reshape(b, n, SSM_GROUPS, SSM_STATE)
    cm = u[..., D_INNER + SSM_BC:].reshape(b, n, SSM_GROUPS, SSM_STATE)
    dt = jax.nn.softplus(dt_raw + dt_bias)
    return xs, dt, bm, cm


def ssm_direction(xbc_ctx, dt_ctx, xbc_lat, dt_lat, conv_w, conv_b, dt_bias, a_log, d_skip):
    a = -jnp.exp(a_log)
    xc, dtc, bc, cc = ssm_prep(xbc_ctx, dt_ctx, conv_w, conv_b, dt_bias)
    h0 = jnp.zeros((xc.shape[0], SSM_HEADS, SSM_HEAD_DIM, SSM_STATE), xc.dtype)
    y_c, h_c = ssd_chunked(xc, dtc, a, bc, cc, h0)
    xl, dtl, bl, cl = ssm_prep(xbc_lat, dt_lat, conv_w, conv_b, dt_bias)
    y_l, _ = ssd_chunked(xl, dtl, a, bl, cl, h_c)
    skip = d_skip[:, None]
    return y_c + skip * xc, y_l + skip * xl


def mamba_mixer(h_lat, h_ctx, w_in, conv_w, conv_b, dt_bias, a_log, d_skip, norm_g, w_out, need_ctx):
    p_lat = h_lat @ w_in
    p_ctx = h_ctx @ w_in
    o_x, o_dt = D_INNER, D_INNER + SSM_CONV_CH
    y_lat = 0.0
    y_ctx = 0.0
    for dr in range(2):
        rev = dr == 1
        sl = slice(o_dt + dr * SSM_HEADS, o_dt + (dr + 1) * SSM_HEADS)
        yc, yl = ssm_direction(flip_seq(p_ctx[..., o_x:o_dt], rev), flip_seq(p_ctx[..., sl], rev),
                               flip_seq(p_lat[..., o_x:o_dt], rev), flip_seq(p_lat[..., sl], rev),
                               conv_w[dr], conv_b[dr], dt_bias[dr], a_log[dr], d_skip[dr])
        y_ctx = y_ctx + flip_seq(yc, rev)
        y_lat = y_lat + flip_seq(yl, rev)

    def out(y, z):
        b, n = y.shape[0], y.shape[1]
        yg = (y.reshape(b, n, D_INNER) * jax.nn.silu(z)).reshape(b, n, SSM_GROUPS, D_INNER // SSM_GROUPS)
        yg = rms_norm(yg, norm_g.reshape(SSM_GROUPS, -1))
        return yg.reshape(b, n, D_INNER) @ w_out

    out_ctx = out(y_ctx, p_ctx[..., :o_x]) if need_ctx else None
    return out(y_lat, p_lat[..., :o_x]), out_ctx


def axial_rope_angles(n_tokens):
    rows = n_tokens // GRID_W
    row = jnp.repeat(jnp.arange(rows), GRID_W).astype(jnp.float32)
    col = jnp.tile(jnp.arange(GRID_W), rows).astype(jnp.float32)
    n_freq = DA_HEAD_DIM // 4
    inv = ROPE_THETA ** (-jnp.arange(n_freq, dtype=jnp.float32) / n_freq)
    return jnp.stack([row[:, None] * inv, col[:, None] * inv], axis=1)


def apply_axial_rope(x, ang):
    sh = x.shape
    xr = x.reshape(sh[:-1] + (2, 2, sh[-1] // 4))
    cos = jnp.cos(ang)[None, :, None, None]
    sin = jnp.sin(ang)[None, :, None, None]
    x1, x2 = xr[..., 0, :], xr[..., 1, :]
    out = jnp.stack([x1 * cos - x2 * sin, x1 * sin + x2 * cos], axis=-2)
    return out.reshape(sh).astype(x.dtype)


def diff_attention(h_lat, h_ctx, w_in, lam_p, norm_g, w_out, layer_idx, need_ctx):
    b, s, _ = h_lat.shape
    scale = DA_HEAD_DIM ** -0.5

    def qkv(h):
        n = h.shape[1]
        p = h @ w_in
        q = p[..., :D_MODEL].reshape(b, n, DA_HEADS, 2, DA_HEAD_DIM) * scale
        k = p[..., D_MODEL:2 * D_MODEL].reshape(b, n, DA_HEADS, 2, DA_HEAD_DIM)
        v = p[..., 2 * D_MODEL:].reshape(b, n, DA_HEADS, 2 * DA_HEAD_DIM)
        return q, k, v

    q_l, k_l, v_l = qkv(h_lat)
    q_c, k_c, v_c = qkv(h_ctx)
    ang = axial_rope_angles(s)
    q_l = apply_axial_rope(q_l, ang)
    k_l = apply_axial_rope(k_l, ang)

    lam_init = 0.8 - 0.6 * math.exp(-0.3 * layer_idx)
    lp = lam_p.astype(jnp.float32)
    lam = jnp.exp(jnp.sum(lp[0] * lp[1])) - jnp.exp(jnp.sum(lp[2] * lp[3])) + lam_init

    def attend(q, k, v):
        sc = jnp.einsum('bqhed,bkhed->bheqk', q, k).astype(jnp.float32)
        pr = jax.nn.softmax(sc, axis=-1)
        w = pr[:, :, 0] - lam * pr[:, :, 1]
        return jnp.einsum('bhqk,bkhv->bqhv', w.astype(v.dtype), v)

    def finish(o):
        o = rms_norm(o, norm_g) * (1.0 - lam_init)
        return o.reshape(o.shape[0], o.shape[1], D_MODEL) @ w_out

    k_all = jnp.concatenate([k_c, k_l], axis=1)
    v_all = jnp.concatenate([v_c, v_l], axis=1)
    qb = jnp.swapaxes(q_l.reshape(b, s // DA_Q_BLOCK, DA_Q_BLOCK, DA_HEADS, 2, DA_HEAD_DIM), 0, 1)
    o_l = lax.map(lambda qq: attend(qq, k_all, v_all), qb)
    o_l = jnp.swapaxes(o_l, 0, 1).reshape(b, s, DA_HEADS, 2 * DA_HEAD_DIM)
    out_ctx = finish(attend(q_c, k_c, v_c)) if need_ctx else None
    return finish(o_l), out_ctx


def conformer_conv(h, w_pw1, b_pw1, dw_w, dw_b, ln_g, ln_b, w_pw2, b_pw2):
    a = h @ w_pw1 + b_pw1
    u = a[..., :D_MODEL] * jax.nn.sigmoid(a[..., D_MODEL:])
    u = dwconv(u, dw_w, dw_b, CONF_KERNEL // 2, CONF_KERNEL // 2)
    u = jax.nn.silu(layer_norm(u, ln_g, ln_b))
    return u @ w_pw2 + b_pw2


def conv_ffn(h, w_up, conv_w, conv_b, w_down):
    u = dwconv(h @ w_up, conv_w, conv_b, FFN_CONV // 2, FFN_CONV // 2)
    return (jax.nn.silu(u[..., :FFN_HIDDEN]) * u[..., FFN_HIDDEN:]) @ w_down


def setup_inputs(seed: int = 0) -> dict:
    key = jax.random.key(seed)
    ks = iter(jax.random.split(key, 48))
    D = D_MODEL

    def nrm(shape, scale=1.0):
        return jax.random.normal(next(ks), shape, jnp.float32) * scale

    def gain(shape):
        return 1.0 + nrm(shape, 0.02)

    dt0 = jnp.exp(jax.random.uniform(next(ks), (N_A, 2, SSM_HEADS), jnp.float32,
                                     minval=math.log(1e-3), maxval=math.log(1e-1)))
    dt_bias = dt0 + jnp.log(-jnp.expm1(-dt0))
    a_log = jnp.log(jax.random.uniform(next(ks), (N_A, 2, SSM_HEADS), jnp.float32, minval=1.0, maxval=16.0))
    return {
        'x': nrm((BATCH, SEQ, D)),
        'c': nrm((BATCH, D)),
        'ctx': nrm((BATCH, CTX_LEN, D)),
        'c_ctx': nrm((D,)),
        'mod_w': nrm((DEPTH, D, N_MOD * D), D ** -0.5),
        'mod_b': nrm((DEPTH, N_MOD * D), 0.01),
        'norm1_g': gain((DEPTH, D)),
        'norm2_g': gain((DEPTH, D)),
        'ffn_w_up': nrm((DEPTH, D, 2 * FFN_HIDDEN), D ** -0.5),
        'ffn_conv_w': nrm((DEPTH, FFN_CONV, 2 * FFN_HIDDEN), FFN_CONV ** -0.5),
        'ffn_conv_b': nrm((DEPTH, 2 * FFN_HIDDEN), 0.01),
        'ffn_w_down': nrm((DEPTH, FFN_HIDDEN, D), FFN_HIDDEN ** -0.5),
        'ssm_w_in': nrm((N_A, D, SSM_IN), D ** -0.5),
        'ssm_conv_w': nrm((N_A, 2, SSM_CONV, SSM_CONV_CH), SSM_CONV ** -0.5),
        'ssm_conv_b': nrm((N_A, 2, SSM_CONV_CH), 0.01),
        'ssm_dt_bias': dt_bias,
        'ssm_a_log': a_log,
        'ssm_d': 1.0 + nrm((N_A, 2, SSM_HEADS), 0.1),
        'ssm_norm_g': gain((N_A, D_INNER)),
        'ssm_w_out': nrm((N_A, D_INNER, D), D_INNER ** -0.5),
        'attn_w_in': nrm((N_B, D, 3 * D), D ** -0.5),
        'attn_lambda': nrm((N_B, 4, DA_HEAD_DIM), 0.1),
        'attn_norm_g': gain((N_B, 2 * DA_HEAD_DIM)),
        'attn_w_out': nrm((N_B, D, D), D ** -0.5),
        'conf_w_pw1': nrm((N_C, D, 2 * D), D ** -0.5),
        'conf_b_pw1': nrm((N_C, 2 * D), 0.01),
        'conf_dw_w': nrm((N_C, CONF_KERNEL, D), CONF_KERNEL ** -0.5),
        'conf_dw_b': nrm((N_C, D), 0.01),
        'conf_ln_g': gain((N_C, D)),
        'conf_ln_b': nrm((N_C, D), 0.01),
        'conf_w_pw2': nrm((N_C, D, D), D ** -0.5),
        'conf_b_pw2': nrm((N_C, D), 0.01),
        'final_g': gain((D,)),
    }


def reference(x, c, ctx, c_ctx, mod_w, mod_b, norm1_g, norm2_g, ffn_w_up, ffn_conv_w, ffn_conv_b,
              ffn_w_down, ssm_w_in, ssm_conv_w, ssm_conv_b, ssm_dt_bias, ssm_a_log, ssm_d, ssm_norm_g,
              ssm_w_out, attn_w_in, attn_lambda, attn_norm_g, attn_w_out, conf_w_pw1, conf_b_pw1,
              conf_dw_w, conf_dw_b, conf_ln_g, conf_ln_b, conf_w_pw2, conf_b_pw2, final_g):
    xc = ctx
    s_lat = jax.nn.silu(c)
    s_ctx = jax.nn.silu(c_ctx)
    for i in range(DEPTH):
        kind, j = i % N_MIXERS, i // N_MIXERS
        need_ctx = i < DEPTH - 1
        ml = jnp.split((s_lat @ mod_w[i] + mod_b[i])[:, None, :], N_MOD, axis=-1)
        mc = jnp.split((s_ctx @ mod_w[i] + mod_b[i])[None, None, :], N_MOD, axis=-1)
        h_lat = modulate(x, norm1_g[i], ml[0], ml[1])
        h_ctx = modulate(xc, norm1_g[i], mc[0], mc[1])
        if kind == 0:
            y_lat, y_ctx = mamba_mixer(h_lat, h_ctx, ssm_w_in[j], ssm_conv_w[j], ssm_conv_b[j], ssm_dt_bias[j],
                                       ssm_a_log[j], ssm_d[j], ssm_norm_g[j], ssm_w_out[j], need_ctx)
        elif kind == 1:
            y_lat, y_ctx = diff_attention(h_lat, h_ctx, attn_w_in[j], attn_lambda[j], attn_norm_g[j],
                                          attn_w_out[j], i, need_ctx)
        else:
            conf = (conf_w_pw1[j], conf_b_pw1[j], conf_dw_w[j], conf_dw_b[j], conf_ln_g[j], conf_ln_b[j],
                    conf_w_pw2[j], conf_b_pw2[j])
            y_lat = conformer_conv(h_lat, *conf)
            y_ctx = conformer_conv(h_ctx, *conf) if need_ctx else None
        ffn = (ffn_w_up[i], ffn_conv_w[i], ffn_conv_b[i], ffn_w_down[i])
        x = x + ml[2] * y_lat
        x = x + ml[5] * conv_ffn(modulate(x, norm2_g[i], ml[3], ml[4]), *ffn)
        if need_ctx:
            xc = xc + mc[2] * y_ctx
            xc = xc + mc[5] * conv_ffn(modulate(xc, norm2_g[i], mc[3], mc[4]), *ffn)
    return rms_norm(x, final_g)
```

```python
import functools
import math

import jax
import jax.numpy as jnp
from jax import lax
from jax.experimental import pallas as pl
from jax.experimental.pallas import tpu as pltpu

F32 = jnp.float32
BF16 = jnp.bfloat16
HIGHEST = lax.Precision.HIGHEST

D = 1024
N_MOD = 6
N_MIXERS = 3
EPS = 1e-6
GRID_W = 64
ROPE_THETA = 10000.0

SSM_HEAD_DIM = 64
SSM_HEADS = 32
SSM_GROUPS = 8
SSM_HPG = SSM_HEADS // SSM_GROUPS
SSM_STATE = 128
SSM_CONV = 4
SSM_CHUNK = 128
D_INNER = SSM_HEADS * SSM_HEAD_DIM
SSM_GW = SSM_HPG * SSM_HEAD_DIM
SSM_BC = SSM_GROUPS * SSM_STATE
SSM_CONV_CH = D_INNER + 2 * SSM_BC
DT_PAD = 128

DA_HEAD_DIM = 64
DA_HEADS = D // (2 * DA_HEAD_DIM)
DA_V = 2 * DA_HEAD_DIM

CONF_KERNEL = 31
CONF_HALO = 16

FFN_HIDDEN = 2816
FFN_CHUNK = 256
FFN_NCHUNK = FFN_HIDDEN // FFN_CHUNK
FFN_HALO = 8

VMEM_LIMIT = 56 * 1024 * 1024
TOKEN_TILE = 512
ATTN_Q_TILE = 256


def _sigmoid(x):
    return 1.0 / (1.0 + jnp.exp(-x))


def _silu(x):
    return x * _sigmoid(x)


def _softplus(x):
    return jnp.maximum(x, 0.0) + jnp.log(1.0 + jnp.exp(-jnp.abs(x)))


def _rms(x):
    return x * lax.rsqrt(jnp.mean(x * x, axis=-1, keepdims=True) + EPS)


def _params(sem, vmem=VMEM_LIMIT):
    return pltpu.CompilerParams(dimension_semantics=sem, vmem_limit_bytes=vmem)


def _mod_spec(layer, row):
    if row is None:
        return pl.BlockSpec((None, None, N_MOD, D), lambda b, *_: (layer, b, 0, 0))
    return pl.BlockSpec((None, None, N_MOD, D), lambda b, *_: (layer, row, 0, 0))


def _const_spec(shape):
    nd = len(shape)
    return pl.BlockSpec(shape, lambda *_: (0,) * nd)


def _tile(n, pref):
    return pref if n % pref == 0 else n


def _mod_kernel(s_ref, w_ref, b_ref, o_ref):
    s = _silu(s_ref[...])
    o_ref[...] = jnp.dot(s, w_ref[...], preferred_element_type=F32, precision=HIGHEST) + b_ref[...]


def _mod_table(c, c_ctx, mod_w, mod_b):
    depth = mod_w.shape[0]
    nb = c.shape[0]
    rows = -(-(nb + 1) // 8) * 8
    s = jnp.zeros((rows, D), F32).at[:nb].set(c).at[nb].set(c_ctx)
    tn = 1536
    out = pl.pallas_call(
        _mod_kernel,
        grid=(depth, N_MOD * D // tn),
        in_specs=[pl.BlockSpec((rows, D), lambda i, n: (0, 0)),
                  pl.BlockSpec((None, D, tn), lambda i, n: (i, 0, n)),
                  pl.BlockSpec((None, 1, tn), lambda i, n: (i, 0, n))],
        out_specs=pl.BlockSpec((None, rows, tn), lambda i, n: (i, 0, n)),
        out_shape=jax.ShapeDtypeStruct((depth, rows, N_MOD * D), F32),
        compiler_params=_params(("arbitrary", "arbitrary")),
    )(s, mod_w, mod_b.reshape(depth, 1, N_MOD * D))
    return out.reshape(depth, rows, N_MOD, D)


def _mamba_in_kernel(x_ref, mod_ref, g_ref, wz_ref, wx_ref, wd_ref, z_ref, xbc_ref, dt_ref):
    gs = g_ref[...] * (1.0 + mod_ref[1:2, :])
    h = (_rms(x_ref[...]) * gs + mod_ref[0:1, :]).astype(BF16)
    for n in range(0, D_INNER, 1024):
        z_ref[:, n:n + 1024] = jnp.dot(h, wz_ref[:, n:n + 1024], preferred_element_type=F32).astype(z_ref.dtype)
    for n in range(0, SSM_CONV_CH, 1024):
        xbc_ref[:, n:n + 1024] = jnp.dot(h, wx_ref[:, n:n + 1024],
                                         preferred_element_type=F32).astype(xbc_ref.dtype)
    dt_ref[...] = jnp.dot(h, wd_ref[...], preferred_element_type=F32)


def _mamba_in(x, mod, layer, row, g, wz, wx, wd):
    nb, t, _ = x.shape
    tm = _tile(t, TOKEN_TILE)
    tok = lambda w: pl.BlockSpec((None, tm, w), lambda b, i: (b, i, 0))
    return pl.pallas_call(
        _mamba_in_kernel,
        grid=(nb, t // tm),
        in_specs=[tok(D), _mod_spec(layer, row), _const_spec((1, D)),
                  _const_spec((D, D_INNER)), _const_spec((D, SSM_CONV_CH)), _const_spec((D, DT_PAD))],
        out_specs=[tok(D_INNER), tok(SSM_CONV_CH), tok(DT_PAD)],
        out_shape=[jax.ShapeDtypeStruct((nb, t, D_INNER), BF16),
                   jax.ShapeDtypeStruct((nb, t, SSM_CONV_CH), BF16),
                   jax.ShapeDtypeStruct((nb, t, DT_PAD), F32)],
        compiler_params=_params(("parallel", "parallel")),
    )(x, mod, g, wz, wx, wd)


def _ssd_kernel(*refs, rev, nc, write_y, combine):
    it = iter(refs)
    xm_ref, xh_ref, dt_ref, cw_ref, cbias_ref, dtb_ref, alog_ref, dsk_ref, h0_ref = (next(it) for _ in range(9))
    if combine:
        yf_ref, z_ref, ng_ref = (next(it) for _ in range(3))
    y_ref = next(it) if write_y else None
    hfin_ref = next(it)
    ht_ref, xbuf_ref = next(it), next(it)

    k = pl.program_id(1)
    c = (nc - 1 - k) if rev else k
    ch, halo = SSM_CHUNK, 16
    off = SSM_HEADS if rev else 0

    @pl.when(k == 0)
    def _():
        ht_ref[...] = h0_ref[...]

    main = xm_ref[...].astype(F32)
    side = xh_ref[...].astype(F32)
    if rev:
        xbuf_ref[0:ch, :] = main
        xbuf_ref[ch:ch + halo, :] = jnp.where(c < nc - 1, side, 0.0)
        starts = [SSM_CONV - 1 - kk for kk in range(SSM_CONV)]
    else:
        xbuf_ref[0:halo, :] = jnp.where(c > 0, side, 0.0)
        xbuf_ref[halo:halo + ch, :] = main
        starts = [halo - (SSM_CONV - 1) + kk for kk in range(SSM_CONV)]

    def conv(c0, w):
        acc = cbias_ref[:, c0:c0 + w]
        for kk in range(SSM_CONV):
            acc = acc + xbuf_ref[starts[kk]:starts[kk] + ch, c0:c0 + w] * cw_ref[kk:kk + 1, c0:c0 + w]
        return _silu(acc)

    dtv = _softplus(dt_ref[...] + dtb_ref[...])
    la = dtv * (-jnp.exp(alog_ref[...]))
    ri = lax.broadcasted_iota(jnp.int32, (ch, ch), 0)
    ci = lax.broadcasted_iota(jnp.int32, (ch, ch), 1)
    keep = (ci >= ri) if rev else (ci <= ri)
    acum = jnp.dot(keep.astype(F32), la, preferred_element_type=F32, precision=HIGHEST)
    last = 0 if rev else ch - 1
    total = acum[last:last + 1, :]
    wcol = jnp.exp(total - acum) * dtv
    offv = jnp.exp(acum)
    acum_t = acum.T
    dt_t = dtv.T

    lane = lax.broadcasted_iota(jnp.int32, (ch, SSM_GW), 1)

    def expand(colmat, g):
        out = jnp.broadcast_to(colmat[:, off + SSM_HPG * g:off + SSM_HPG * g + 1], (ch, SSM_GW))
        for j in range(1, SSM_HPG):
            col = off + SSM_HPG * g + j
            out = jnp.where(lane >= j * SSM_HEAD_DIM, jnp.broadcast_to(colmat[:, col:col + 1], (ch, SSM_GW)), out)
        return out

    for g in range(SSM_GROUPS):
        xg = conv(g * SSM_GW, SSM_GW)
        bg = conv(D_INNER + g * SSM_STATE, SSM_STATE)
        bg16 = bg.astype(BF16)
        xw = (xg * expand(wcol, g)).astype(BF16)
        st_t = lax.dot_general(bg16, xw, (((0,), (0,)), ((), ())), preferred_element_type=F32)
        sg = expand(offv, g)
        ht_old = ht_ref[g]
        if write_y:
            cg16 = conv(D_INNER + SSM_BC + g * SSM_STATE, SSM_STATE).astype(BF16)
            cbm = lax.dot_general(cg16, bg16, (((1,), (1,)), ((), ())), preferred_element_type=F32)
            yoff = jnp.dot(cg16, ht_old.astype(BF16), preferred_element_type=F32)
            ms, xs = [], []
            xg16 = xg.astype(BF16)
            for j in range(SSM_HPG):
                col = off + SSM_HPG * g + j
                seg = acum[:, col:col + 1] - acum_t[col:col + 1, :]
                lmat = jnp.exp(jnp.where(keep, seg, -jnp.inf))
                ms.append((cbm * lmat * dt_t[col:col + 1, :]).astype(BF16))
                in_head = (lane >= j * SSM_HEAD_DIM) & (lane < (j + 1) * SSM_HEAD_DIM)
                xs.append(jnp.where(in_head, xg16, jnp.zeros_like(xg16)))
            ydiag = jnp.dot(jnp.concatenate(ms, axis=1), jnp.concatenate(xs, axis=0),
                            preferred_element_type=F32)
            y = ydiag + yoff * sg + dsk_ref[:, g * SSM_GW:(g + 1) * SSM_GW] * xg
            cols = slice(g * SSM_GW, (g + 1) * SSM_GW)
            if combine:
                y = (y + yf_ref[:, cols].astype(F32)) * _silu(z_ref[:, cols].astype(F32))
                y = _rms(y) * ng_ref[:, cols]
            y_ref[:, cols] = y.astype(y_ref.dtype)
        ht_ref[g] = ht_old * sg[last:last + 1, :] + st_t

    @pl.when(k == nc - 1)
    def _():
        hfin_ref[...] = ht_ref[...]


def _ssd(xbc, dt, h0, conv_w, conv_b, dt_bias, a_log, d_skip, *, rev, write_y, combine=None, y_dtype=F32):
    nb, t, _ = xbc.shape
    nc = t // SSM_CHUNK
    halo = 16
    hb = SSM_CHUNK // halo
    if rev:
        cidx = lambda k: nc - 1 - k
        hidx = lambda k: jnp.minimum((nc - k) * hb, t // halo - 1)
    else:
        cidx = lambda k: k
        hidx = lambda k: jnp.maximum(k * hb - 1, 0)
    chunk = lambda w: pl.BlockSpec((None, SSM_CHUNK, w), lambda b, k: (b, cidx(k), 0))
    state = pl.BlockSpec((None, SSM_GROUPS, SSM_STATE, SSM_GW), lambda b, k: (b, 0, 0, 0))
    in_specs = [chunk(SSM_CONV_CH),
                pl.BlockSpec((None, halo, SSM_CONV_CH), lambda b, k: (b, hidx(k), 0)),
                chunk(DT_PAD),
                _const_spec((SSM_CONV, SSM_CONV_CH)), _const_spec((1, SSM_CONV_CH)),
                _const_spec((1, DT_PAD)), _const_spec((1, DT_PAD)), _const_spec((1, D_INNER)), state]
    args = [xbc, xbc, dt, conv_w, conv_b, dt_bias, a_log, d_skip, h0]
    out_specs, out_shape = [], []
    if combine is not None:
        yf, z, ng = combine
        in_specs += [chunk(D_INNER), chunk(D_INNER), _const_spec((1, D_INNER))]
        args += [yf, z, ng]
    if write_y:
        out_specs.append(chunk(D_INNER))
        out_shape.append(jax.ShapeDtypeStruct((nb, t, D_INNER), y_dtype))
    out_specs.append(state)
    out_shape.append(jax.ShapeDtypeStruct((nb, SSM_GROUPS, SSM_STATE, SSM_GW), F32))
    outs = pl.pallas_call(
        functools.partial(_ssd_kernel, rev=rev, nc=nc, write_y=write_y, combine=combine is not None),
        grid=(nb, nc),
        in_specs=in_specs, out_specs=out_specs, out_shape=out_shape,
        scratch_shapes=[pltpu.VMEM((SSM_GROUPS, SSM_STATE, SSM_GW), F32),
                        pltpu.VMEM((SSM_CHUNK + halo, SSM_CONV_CH), F32)],
        compiler_params=_params(("parallel", "arbitrary")),
    )(*args)
    return outs if write_y else (None, outs[0])


def _resid_kernel(a_ref, w_ref, b_ref, x_ref, mod_ref, o_ref, *, gate_row):
    y = jnp.dot(a_ref[...], w_ref[...], preferred_element_type=F32) + b_ref[...]
    o_ref[...] = x_ref[...] + mod_ref[gate_row:gate_row + 1, :] * y


def _resid_matmul(a, w, bias, x, mod, layer, row, gate_row):
    nb, t, kdim = a.shape
    tm = _tile(t, TOKEN_TILE)
    tok = lambda wd: pl.BlockSpec((None, tm, wd), lambda b, i: (b, i, 0))
    return pl.pallas_call(
        functools.partial(_resid_kernel, gate_row=gate_row),
        grid=(nb, t // tm),
        in_specs=[tok(kdim), _const_spec((kdim, D)), _const_spec((1, D)), tok(D), _mod_spec(layer, row)],
        out_specs=tok(D),
        out_shape=jax.ShapeDtypeStruct((nb, t, D), F32),
        compiler_params=_params(("parallel", "parallel")),
    )(a, w, bias, x, mod)


def _mamba_layer(x, xc, mod, layer, nb, g1, p, need_ctx):
    wz, wx, wd, cw, cbias, dtb, alog, dsk, ng, wout = p
    z_l, xbc_l, dt_l = _mamba_in(x, mod, layer, None, g1, wz, wx, wd)
    z_c, xbc_c, dt_c = _mamba_in(xc, mod, layer, nb, g1, wz, wx, wd)
    h0 = jnp.zeros((x.shape[0], SSM_GROUPS, SSM_STATE, SSM_GW), F32)
    dirp = lambda d: (cw[d], cbias[d], dtb, alog, dsk[d])
    yf_c, hf = _ssd(xbc_c, dt_c, h0, *dirp(0), rev=False, write_y=need_ctx)
    yf_l, _ = _ssd(xbc_l, dt_l, hf, *dirp(0), rev=False, write_y=True)
    yn_c, hb = _ssd(xbc_c, dt_c, h0, *dirp(1), rev=True, write_y=need_ctx,
                    combine=(yf_c, z_c, ng) if need_ctx else None, y_dtype=BF16)
    yn_l, _ = _ssd(xbc_l, dt_l, hb, *dirp(1), rev=True, write_y=True, combine=(yf_l, z_l, ng), y_dtype=BF16)
    zero_b = jnp.zeros((1, D), F32)
    x = _resid_matmul(yn_l, wout, zero_b, x, mod, layer, None, 2)
    if need_ctx:
        xc = _resid_matmul(yn_c, wout, zero_b, xc, mod, layer, nb, 2)
    return x, xc


def _qkv_kernel(*refs, rope):
    if rope:
        x_ref, mod_ref, g_ref, wq_ref, wk_ref, wv_ref, cos_ref, sin_ref, q_ref, k_ref, v_ref = refs
    else:
        x_ref, mod_ref, g_ref, wq_ref, wk_ref, wv_ref, q_ref, k_ref, v_ref = refs
    gs = g_ref[...] * (1.0 + mod_ref[1:2, :])
    h = (_rms(x_ref[...]) * gs + mod_ref[0:1, :]).astype(BF16)
    tm = h.shape[0]
    if rope:
        cos, sin = cos_ref[...], sin_ref[...]
        lane = lax.broadcasted_iota(jnp.int32, (tm, 128), 1)
        first = (lane % 32) < 16

    def rot(a_ref, w_ref, scale):
        for j in range(D // 128):
            blk = jnp.dot(h, w_ref[:, j * 128:(j + 1) * 128], preferred_element_type=F32)
            if scale != 1.0:
                blk = blk * scale
            if rope:
                partner = jnp.where(first, pltpu.roll(blk, 112, 1), pltpu.roll(blk, 16, 1))
                blk = blk * cos + partner * sin
            a_ref[:, j * 128:(j + 1) * 128] = blk.astype(a_ref.dtype)

    rot(q_ref, wq_ref, DA_HEAD_DIM ** -0.5)
    rot(k_ref, wk_ref, 1.0)
    v_ref[...] = jnp.dot(h, wv_ref[...], preferred_element_type=F32).astype(v_ref.dtype)


def _rope_tables(s):
    lane = jnp.arange(128)
    pos = jnp.arange(s)
    coord = jnp.where(((lane % 64) // 32 == 0)[None, :], (pos // GRID_W)[:, None], (pos % GRID_W)[:, None])
    n_freq = DA_HEAD_DIM // 4
    inv = ROPE_THETA ** (-(lane % n_freq).astype(F32) / n_freq)
    ang = coord.astype(F32) * inv[None, :]
    sign = jnp.where((lane % 32) < 16, -1.0, 1.0)
    return jnp.cos(ang), jnp.sin(ang) * sign[None, :]


def _qkv(x, mod, layer, row, g, wq, wk, wv, tables):
    nb, t, _ = x.shape
    tm = _tile(t, TOKEN_TILE)
    tok = lambda: pl.BlockSpec((None, tm, D), lambda b, i: (b, i, 0))
    in_specs = [tok(), _mod_spec(layer, row), _const_spec((1, D))] + [_const_spec((D, D))] * 3
    args = [x, mod, g, wq, wk, wv]
    if tables is not None:
        in_specs += [pl.BlockSpec((tm, 128), lambda b, i: (i, 0))] * 2
        args += list(tables)
    return pl.pallas_call(
        functools.partial(_qkv_kernel, rope=tables is not None),
        grid=(nb, t // tm),
        in_specs=in_specs, out_specs=[tok()] * 3,
        out_shape=[jax.ShapeDtypeStruct((nb, t, D), BF16)] * 3,
        compiler_params=_params(("parallel", "parallel")),
    )(*args)


def _attn_kernel(*refs, nseg, lam_init):
    lam_ref, ng_ref, q_ref = refs[:3]
    kv = refs[3:3 + 2 * nseg]
    o_ref = refs[3 + 2 * nseg]
    q = q_ref[...]
    tq = q.shape[0]
    lane = lax.broadcasted_iota(jnp.int32, q.shape, 1)
    zero = jnp.zeros_like(q)
    qq = jnp.concatenate([jnp.where(lane < DA_HEAD_DIM, q, zero), jnp.where(lane >= DA_HEAD_DIM, q, zero)], axis=0)
    scores = [lax.dot_general(qq, kv[2 * i][...], (((1,), (1,)), ((), ())), preferred_element_type=F32)
              for i in range(nseg)]
    m = scores[0].max(axis=-1, keepdims=True)
    for s in scores[1:]:
        m = jnp.maximum(m, s.max(axis=-1, keepdims=True))
    den = jnp.zeros_like(m)
    acc = jnp.zeros((2 * tq, DA_V), F32)
    for i in range(nseg):
        p = jnp.exp(scores[i] - m)
        den = den + p.sum(axis=-1, keepdims=True)
        acc = acc + jnp.dot(p.astype(BF16), kv[2 * i + 1][...], preferred_element_type=F32)
    acc = acc / den
    lp = lam_ref[...]
    lam = (jnp.exp(jnp.sum(lp[0:1] * lp[1:2], axis=-1, keepdims=True))
           - jnp.exp(jnp.sum(lp[2:3] * lp[3:4], axis=-1, keepdims=True)) + lam_init)
    o = acc[:tq] - lam * acc[tq:]
    o_ref[...] = (_rms(o) * ng_ref[...] * (1.0 - lam_init)).astype(o_ref.dtype)


def _attention(q, segs, lam_p, ng, lam_init):
    nb, t, _ = q.shape
    tq = _tile(t, ATTN_Q_TILE)
    in_specs = [_const_spec((4, DA_HEAD_DIM)), _const_spec((1, DA_V)),
                pl.BlockSpec((None, tq, DA_V), lambda b, h, i: (b, i, h))]
    args = [lam_p, ng, q]
    for k, v in segs:
        tk = k.shape[1]
        in_specs += [pl.BlockSpec((None, tk, DA_V), lambda b, h, i: (b, 0, h))] * 2
        args += [k, v]
    return pl.pallas_call(
        functools.partial(_attn_kernel, nseg=len(segs), lam_init=lam_init),
        grid=(nb, DA_HEADS, t // tq),
        in_specs=in_specs,
        out_specs=pl.BlockSpec((None, tq, DA_V), lambda b, h, i: (b, i, h)),
        out_shape=jax.ShapeDtypeStruct((nb, t, D), BF16),
        compiler_params=_params(("parallel", "parallel", "arbitrary")),
    )(*args)


def _attn_layer(x, xc, mod, layer, nb, g1, p, need_ctx):
    wq, wk, wv, lam_p, ng, wout = p
    lam_init = 0.8 - 0.6 * math.exp(-0.3 * layer)
    q_l, k_l, v_l = _qkv(x, mod, layer, None, g1, wq, wk, wv, _rope_tables(x.shape[1]))
    q_c, k_c, v_c = _qkv(xc, mod, layer, nb, g1, wq, wk, wv, None)
    zero_b = jnp.zeros((1, D), F32)
    o_l = _attention(q_l, [(k_c, v_c), (k_l, v_l)], lam_p, ng, lam_init)
    x = _resid_matmul(o_l, wout, zero_b, x, mod, layer, None, 2)
    if need_ctx:
        o_c = _attention(q_c, [(k_c, v_c)], lam_p, ng, lam_init)
        xc = _resid_matmul(o_c, wout, zero_b, xc, mod, layer, nb, 2)
    return x, xc


def _conf_kernel(xm_ref, xp_ref, xn_ref, mod_ref, g_ref, w1_ref, b1_ref, dw_ref, db_ref, lg_ref, lb_ref,
                 w2_ref, b2_ref, o_ref, u_ref, c_ref, *, tm, nt):
    t = pl.program_id(1)
    hl = CONF_HALO
    gs = g_ref[...] * (1.0 + mod_ref[1:2, :])
    shift = mod_ref[0:1, :]

    def glu(x, valid):
        h = (_rms(x) * gs + shift).astype(BF16)
        a = jnp.dot(h, w1_ref[...], preferred_element_type=F32) + b1_ref[...]
        u = a[:, :D] * _sigmoid(a[:, D:])
        return u if valid is None else jnp.where(valid, u, 0.0)

    u_ref[0:hl, :] = glu(xp_ref[...], t > 0)
    u_ref[hl:hl + tm, :] = glu(xm_ref[...], None)
    u_ref[hl + tm:, :] = glu(xn_ref[...], t < nt - 1)

    rb, cw = 64, 256
    pad = CONF_KERNEL // 2
    for r0 in range(0, tm, rb):
        for c0 in range(0, D, cw):
            acc = jnp.broadcast_to(db_ref[:, c0:c0 + cw], (rb, cw))
            for kk in range(CONF_KERNEL):
                s = hl - pad + kk + r0
                acc = acc + u_ref[s:s + rb, c0:c0 + cw] * dw_ref[kk:kk + 1, c0:c0 + cw]
            c_ref[r0:r0 + rb, c0:c0 + cw] = acc

    cv = c_ref[...]
    mu = jnp.mean(cv, axis=-1, keepdims=True)
    xc = cv - mu
    ln = xc * lax.rsqrt(jnp.mean(xc * xc, axis=-1, keepdims=True) + EPS) * lg_ref[...] + lb_ref[...]
    y = jnp.dot(_silu(ln).astype(BF16), w2_ref[...], preferred_element_type=F32) + b2_ref[...]
    o_ref[...] = xm_ref[...] + mod_ref[2:3, :] * y


def _conformer(x, mod, layer, row, g, p):
    w1, b1, dw, db, lg, lb, w2, b2 = p
    nb, t, _ = x.shape
    tm = _tile(t, 256)
    nt = t // tm
    hl = CONF_HALO
    r = tm // hl
    return pl.pallas_call(
        functools.partial(_conf_kernel, tm=tm, nt=nt),
        grid=(nb, nt),
        in_specs=[pl.BlockSpec((None, tm, D), lambda b, i: (b, i, 0)),
                  pl.BlockSpec((None, hl, D), lambda b, i: (b, jnp.maximum(i * r - 1, 0), 0)),
                  pl.BlockSpec((None, hl, D), lambda b, i: (b, jnp.minimum((i + 1) * r, t // hl - 1), 0)),
                  _mod_spec(layer, row), _const_spec((1, D)),
                  _const_spec((D, 2 * D)), _const_spec((1, 2 * D)),
                  _const_spec((CONF_KERNEL, D)), _const_spec((1, D)), _const_spec((1, D)), _const_spec((1, D)),
                  _const_spec((D, D)), _const_spec((1, D))],
        out_specs=pl.BlockSpec((None, tm, D), lambda b, i: (b, i, 0)),
        out_shape=jax.ShapeDtypeStruct((nb, t, D), F32),
        scratch_shapes=[pltpu.VMEM((tm + 2 * hl, D), F32), pltpu.VMEM((tm, D), F32)],
        compiler_params=_params(("parallel", "parallel")),
    )(x, x, x, mod, g, w1, b1, dw, db, lg, lb, w2, b2)


def _ffn_kernel(*refs, tm, nt, final):
    if final:
        (xm_ref, xp_ref, xn_ref, mod_ref, g_ref, wu_ref, cw_ref, cb_ref, wd_ref, fg_ref,
         o_ref, h_ref, u_ref, acc_ref) = refs
    else:
        (xm_ref, xp_ref, xn_ref, mod_ref, g_ref, wu_ref, cw_ref, cb_ref, wd_ref,
         o_ref, h_ref, u_ref, acc_ref) = refs
    t = pl.program_id(1)
    hl = FFN_HALO
    gs = g_ref[...] * (1.0 + mod_ref[4:5, :])
    shift = mod_ref[3:4, :]
    mod = lambda x: _rms(x) * gs + shift
    hp = jnp.where(t > 0, mod(xp_ref[...]), 0.0)
    hn = jnp.where(t < nt - 1, mod(xn_ref[...]), 0.0)
    h_ref[...] = jnp.concatenate([hp, mod(xm_ref[...]), hn], axis=0).astype(BF16)
    acc_ref[...] = jnp.zeros_like(acc_ref)

    def body(j, carry):
        u_ref[...] = jnp.dot(h_ref[...], wu_ref[j], preferred_element_type=F32)
        w = cw_ref[j]
        cv = (cb_ref[j] + w[0:1, :] * u_ref[hl - 1:hl - 1 + tm, :] + w[1:2, :] * u_ref[hl:hl + tm, :]
              + w[2:3, :] * u_ref[hl + 1:hl + 1 + tm, :])
        act = (_silu(cv[:, :FFN_CHUNK]) * cv[:, FFN_CHUNK:]).astype(BF16)
        acc_ref[...] += jnp.dot(act, wd_ref[j], preferred_element_type=F32)
        return carry

    lax.fori_loop(0, FFN_NCHUNK, body, 0)
    out = xm_ref[...] + mod_ref[5:6, :] * acc_ref[...]
    if final:
        out = _rms(out) * fg_ref[...]
    o_ref[...] = out


def _ffn(x, mod, layer, row, g, p, final_g=None):
    wu, cw, cb, wd = p
    nb, t, _ = x.shape
    tm = _tile(t, TOKEN_TILE)
    nt = t // tm
    hl = FFN_HALO
    r = tm // hl
    final = final_g is not None
    in_specs = [pl.BlockSpec((None, tm, D), lambda b, i: (b, i, 0)),
                pl.BlockSpec((None, hl, D), lambda b, i: (b, jnp.maximum(i * r - 1, 0), 0)),
                pl.BlockSpec((None, hl, D), lambda b, i: (b, jnp.minimum((i + 1) * r, t // hl - 1), 0)),
                _mod_spec(layer, row), _const_spec((1, D)),
                _const_spec((FFN_NCHUNK, D, 2 * FFN_CHUNK)), _const_spec((FFN_NCHUNK, 3, 2 * FFN_CHUNK)),
                _const_spec((FFN_NCHUNK, 1, 2 * FFN_CHUNK)), _const_spec((FFN_NCHUNK, FFN_CHUNK, D))]
    args = [x, x, x, mod, g, wu, cw, cb, wd]
    if final:
        in_specs.append(_const_spec((1, D)))
        args.append(final_g)
    return pl.pallas_call(
        functools.partial(_ffn_kernel, tm=tm, nt=nt, final=final),
        grid=(nb, nt),
        in_specs=in_specs,
        out_specs=pl.BlockSpec((None, tm, D), lambda b, i: (b, i, 0)),
        out_shape=jax.ShapeDtypeStruct((nb, t, D), F32),
        scratch_shapes=[pltpu.VMEM((tm + 2 * hl, D), BF16), pltpu.VMEM((tm + 2 * hl, 2 * FFN_CHUNK), F32),
                        pltpu.VMEM((tm, D), F32)],
        compiler_params=_params(("parallel", "parallel")),
    )(*args)


def _ffn_weights(w_up, conv_w, conv_b, w_down):
    def pair(a):
        lead = a.shape[:-1]
        a = a.reshape(lead + (2, FFN_NCHUNK, FFN_CHUNK))
        a = jnp.moveaxis(a, -2, 0)
        return a.reshape((FFN_NCHUNK,) + lead + (2 * FFN_CHUNK,))
    return (pair(w_up).astype(BF16), pair(conv_w), pair(conv_b[None, :]),
            w_down.reshape(FFN_NCHUNK, FFN_CHUNK, D).astype(BF16))


def kernel(x, c, ctx, c_ctx, mod_w, mod_b, norm1_g, norm2_g, ffn_w_up, ffn_conv_w, ffn_conv_b, ffn_w_down,
           ssm_w_in, ssm_conv_w, ssm_conv_b, ssm_dt_bias, ssm_a_log, ssm_d, ssm_norm_g, ssm_w_out, attn_w_in,
           attn_lambda, attn_norm_g, attn_w_out, conf_w_pw1, conf_b_pw1, conf_dw_w, conf_dw_b, conf_ln_g,
           conf_ln_b, conf_w_pw2, conf_b_pw2, final_g):
    depth = mod_w.shape[0]
    nb = x.shape[0]
    mod = _mod_table(c, c_ctx, mod_w, mod_b)
    xc = ctx
    row = lambda a: a.reshape(1, -1)
    pad_dt = lambda a: jnp.pad(a.reshape(1, -1), ((0, 0), (0, DT_PAD - 2 * SSM_HEADS)))
    for i in range(depth):
        kind, j = i % N_MIXERS, i // N_MIXERS
        need_ctx = i < depth - 1
        g1 = row(norm1_g[i])
        if kind == 0:
            w_in = ssm_w_in[j]
            o_x, o_dt = D_INNER, D_INNER + SSM_CONV_CH
            wd = jnp.pad(w_in[:, o_dt:], ((0, 0), (0, DT_PAD - 2 * SSM_HEADS)))
            p = (w_in[:, :o_x].astype(BF16), w_in[:, o_x:o_dt].astype(BF16), wd.astype(BF16),
                 ssm_conv_w[j], ssm_conv_b[j][:, None, :], pad_dt(ssm_dt_bias[j]), pad_dt(ssm_a_log[j]),
                 jnp.repeat(ssm_d[j], SSM_HEAD_DIM, axis=-1)[:, None, :], row(ssm_norm_g[j]),
                 ssm_w_out[j].astype(BF16))
            x, xc = _mamba_layer(x, xc, mod, i, nb, g1, p, need_ctx)
        elif kind == 1:
            w_in = attn_w_in[j].astype(BF16)
            p = (w_in[:, :D], w_in[:, D:2 * D], w_in[:, 2 * D:], attn_lambda[j], row(attn_norm_g[j]),
                 attn_w_out[j].astype(BF16))
            x, xc = _attn_layer(x, xc, mod, i, nb, g1, p, need_ctx)
        else:
            p = (conf_w_pw1[j].astype(BF16), row(conf_b_pw1[j]), conf_dw_w[j], row(conf_dw_b[j]),
                 row(conf_ln_g[j]), row(conf_ln_b[j]), conf_w_pw2[j].astype(BF16), row(conf_b_pw2[j]))
            x = _conformer(x, mod, i, None, g1, p)
            if need_ctx:
                xc = _conformer(xc, mod, i, nb, g1, p)
        fp = _ffn_weights(ffn_w_up[i], ffn_conv_w[i], ffn_conv_b[i], ffn_w_down[i])
        g2 = row(norm2_g[i])
        x = _ffn(x, mod, i, None, g2, fp, final_g=row(final_g) if i == depth - 1 else None)
        if need_ctx:
            xc = _ffn(xc, mod, i, nb, g2, fp)
    return x
```

```python
import functools
import math

import jax
import jax.numpy as jnp
from jax import lax
from jax.experimental import pallas as pl
from jax.experimental.pallas import tpu as pltpu

F32 = jnp.float32
BF16 = jnp.bfloat16
HIGHEST = lax.Precision.HIGHEST

D = 1024
N_MOD = 6
N_MIXERS = 3
EPS = 1e-6
GRID_W = 64
ROPE_THETA = 10000.0

SSM_HEAD_DIM = 64
SSM_HEADS = 32
SSM_GROUPS = 8
SSM_HPG = SSM_HEADS // SSM_GROUPS
SSM_STATE = 128
SSM_CONV = 4
SSM_CHUNK = 128
D_INNER = SSM_HEADS * SSM_HEAD_DIM
SSM_GW = SSM_HPG * SSM_HEAD_DIM
SSM_BC = SSM_GROUPS * SSM_STATE
SSM_CONV_CH = D_INNER + 2 * SSM_BC
DT_PAD = 128

DA_HEAD_DIM = 64
DA_HEADS = D // (2 * DA_HEAD_DIM)
DA_V = 2 * DA_HEAD_DIM

CONF_KERNEL = 31
CONF_HALO = 16

FFN_HIDDEN = 2816
FFN_CHUNK = 256
FFN_NCHUNK = FFN_HIDDEN // FFN_CHUNK
FFN_HALO = 8

VMEM_LIMIT = 56 * 1024 * 1024
TOKEN_TILE = 512
ATTN_Q_TILE = 256


def _sigmoid(x):
    return 1.0 / (1.0 + jnp.exp(-x))


def _silu(x):
    return x * _sigmoid(x)


def _softplus(x):
    return jnp.maximum(x, 0.0) + jnp.log(1.0 + jnp.exp(-jnp.abs(x)))


def _rms(x):
    return x * lax.rsqrt(jnp.mean(x * x, axis=-1, keepdims=True) + EPS)


def _params(sem, vmem=VMEM_LIMIT):
    return pltpu.CompilerParams(dimension_semantics=sem, vmem_limit_bytes=vmem)


def _mod_spec(layer, row):
    if row is None:
        return pl.BlockSpec((None, None, N_MOD, D), lambda b, *_: (layer, b, 0, 0))
    return pl.BlockSpec((None, None, N_MOD, D), lambda b, *_: (layer, row, 0, 0))


def _const_spec(shape):
    nd = len(shape)
    return pl.BlockSpec(shape, lambda *_: (0,) * nd)


def _tile(n, pref):
    return pref if n % pref == 0 else n


def _mod_kernel(s_ref, w_ref, b_ref, o_ref):
    s = _silu(s_ref[...])
    o_ref[...] = jnp.dot(s, w_ref[...], preferred_element_type=F32, precision=HIGHEST) + b_ref[...]


def _mod_table(c, c_ctx, mod_w, mod_b):
    depth = mod_w.shape[0]
    nb = c.shape[0]
    rows = -(-(nb + 1) // 8) * 8
    s = jnp.zeros((rows, D), F32).at[:nb].set(c).at[nb].set(c_ctx)
    tn = 1536
    out = pl.pallas_call(
        _mod_kernel,
        grid=(depth, N_MOD * D // tn),
        in_specs=[pl.BlockSpec((rows, D), lambda i, n: (0, 0)),
                  pl.BlockSpec((None, D, tn), lambda i, n: (i, 0, n)),
                  pl.BlockSpec((None, 1, tn), lambda i, n: (i, 0, n))],
        out_specs=pl.BlockSpec((None, rows, tn), lambda i, n: (i, 0, n)),
        out_shape=jax.ShapeDtypeStruct((depth, rows, N_MOD * D), F32),
        compiler_params=_params(("arbitrary", "arbitrary")),
    )(s, mod_w, mod_b.reshape(depth, 1, N_MOD * D))
    return out.reshape(depth, rows, N_MOD, D)


def _mamba_in_kernel(x_ref, mod_ref, g_ref, wz_ref, wx_ref, wd_ref, z_ref, xbc_ref, dt_ref):
    gs = g_ref[...] * (1.0 + mod_ref[1:2, :])
    h = (_rms(x_ref[...]) * gs + mod_ref[0:1, :]).astype(BF16)
    for n in range(0, D_INNER, 1024):
        z_ref[:, n:n + 1024] = jnp.dot(h, wz_ref[:, n:n + 1024], preferred_element_type=F32).astype(z_ref.dtype)
    for n in range(0, SSM_CONV_CH, 1024):
        xbc_ref[:, n:n + 1024] = jnp.dot(h, wx_ref[:, n:n + 1024],
                                         preferred_element_type=F32).astype(xbc_ref.dtype)
    dt_ref[...] = jnp.dot(h, wd_ref[...], preferred_element_type=F32)


def _mamba_in(x, mod, layer, row, g, wz, wx, wd):
    nb, t, _ = x.shape
    tm = _tile(t, TOKEN_TILE)
    tok = lambda w: pl.BlockSpec((None, tm, w), lambda b, i: (b, i, 0))
    return pl.pallas_call(
        _mamba_in_kernel,
        grid=(nb, t // tm),
        in_specs=[tok(D), _mod_spec(layer, row), _const_spec((1, D)),
                  _const_spec((D, D_INNER)), _const_spec((D, SSM_CONV_CH)), _const_spec((D, DT_PAD))],
        out_specs=[tok(D_INNER), tok(SSM_CONV_CH), tok(DT_PAD)],
        out_shape=[jax.ShapeDtypeStruct((nb, t, D_INNER), BF16),
                   jax.ShapeDtypeStruct((nb, t, SSM_CONV_CH), BF16),
                   jax.ShapeDtypeStruct((nb, t, DT_PAD), F32)],
        compiler_params=_params(("parallel", "parallel")),
    )(x, mod, g, wz, wx, wd)


def _ssd_kernel(*refs, rev, nc, write_y, combine):
    it = iter(refs)
    xm_ref, xh_ref, dt_ref, cw_ref, cbias_ref, dtb_ref, alog_ref, dsk_ref, h0_ref = (next(it) for _ in range(9))
    if combine:
        yf_ref, z_ref, ng_ref = (next(it) for _ in range(3))
    y_ref = next(it) if write_y else None
    hfin_ref = next(it)
    ht_ref, xbuf_ref = next(it), next(it)

    k = pl.program_id(1)
    c = (nc - 1 - k) if rev else k
    ch, halo = SSM_CHUNK, 16
    off = SSM_HEADS if rev else 0

    @pl.when(k == 0)
    def _():
        ht_ref[...] = h0_ref[...]

    main = xm_ref[...].astype(F32)
    side = xh_ref[...].astype(F32)
    if rev:
        xbuf_ref[0:ch, :] = main
        xbuf_ref[ch:ch + halo, :] = jnp.where(c < nc - 1, side, 0.0)
        starts = [SSM_CONV - 1 - kk for kk in range(SSM_CONV)]
    else:
        xbuf_ref[0:halo, :] = jnp.where(c > 0, side, 0.0)
        xbuf_ref[halo:halo + ch, :] = main
        starts = [halo - (SSM_CONV - 1) + kk for kk in range(SSM_CONV)]

    def conv(c0, w):
        acc = cbias_ref[:, c0:c0 + w]
        for kk in range(SSM_CONV):
            acc = acc + xbuf_ref[starts[kk]:starts[kk] + ch, c0:c0 + w] * cw_ref[kk:kk + 1, c0:c0 + w]
        return _silu(acc)

    dtv = _softplus(dt_ref[...] + dtb_ref[...])
    la = dtv * (-jnp.exp(alog_ref[...]))
    ri = lax.broadcasted_iota(jnp.int32, (ch, ch), 0)
    ci = lax.broadcasted_iota(jnp.int32, (ch, ch), 1)
    keep = (ci >= ri) if rev else (ci <= ri)
    acum = jnp.dot(keep.astype(F32), la, preferred_element_type=F32, precision=HIGHEST)
    last = 0 if rev else ch - 1
    total = acum[last:last + 1, :]
    wcol = jnp.exp(total - acum) * dtv
    offv = jnp.exp(acum)
    acum_t = acum.T
    dt_t = dtv.T

    lane = lax.broadcasted_iota(jnp.int32, (ch, SSM_GW), 1)

    def expand(colmat, g):
        out = jnp.broadcast_to(colmat[:, off + SSM_HPG * g:off + SSM_HPG * g + 1], (ch, SSM_GW))
        for j in range(1, SSM_HPG):
            col = off + SSM_HPG * g + j
            out = jnp.where(lane >= j * SSM_HEAD_DIM, jnp.broadcast_to(colmat[:, col:col + 1], (ch, SSM_GW)), out)
        return out

    for g in range(SSM_GROUPS):
        xg = conv(g * SSM_GW, SSM_GW)
        bg = conv(D_INNER + g * SSM_STATE, SSM_STATE)
        bg16 = bg.astype(BF16)
        xw = (xg * expand(wcol, g)).astype(BF16)
        st_t = lax.dot_general(bg16, xw, (((0,), (0,)), ((), ())), preferred_element_type=F32)
        sg = expand(offv, g)
        ht_old = ht_ref[g]
        if write_y:
            cg16 = conv(D_INNER + SSM_BC + g * SSM_STATE, SSM_STATE).astype(BF16)
            cbm = lax.dot_general(cg16, bg16, (((1,), (1,)), ((), ())), preferred_element_type=F32)
            yoff = jnp.dot(cg16, ht_old.astype(BF16), preferred_element_type=F32)
            ms, xs = [], []
            xg16 = xg.astype(BF16)
            for j in range(SSM_HPG):
                col = off + SSM_HPG * g + j
                seg = acum[:, col:col + 1] - acum_t[col:col + 1, :]
                lmat = jnp.exp(jnp.where(keep, seg, -jnp.inf))
                ms.append((cbm * lmat * dt_t[col:col + 1, :]).astype(BF16))
                in_head = (lane >= j * SSM_HEAD_DIM) & (lane < (j + 1) * SSM_HEAD_DIM)
                xs.append(jnp.where(in_head, xg16, jnp.zeros_like(xg16)))
            ydiag = jnp.dot(jnp.concatenate(ms, axis=1), jnp.concatenate(xs, axis=0),
                            preferred_element_type=F32)
            y = ydiag + yoff * sg + dsk_ref[:, g * SSM_GW:(g + 1) * SSM_GW] * xg
            cols = slice(g * SSM_GW, (g + 1) * SSM_GW)
            if combine:
                y = (y + yf_ref[:, cols].astype(F32)) * _silu(z_ref[:, cols].astype(F32))
                y = _rms(y) * ng_ref[:, cols]
            y_ref[:, cols] = y.astype(y_ref.dtype)
        ht_ref[g] = ht_old * sg[last:last + 1, :] + st_t

    @pl.when(k == nc - 1)
    def _():
        hfin_ref[...] = ht_ref[...]


def _ssd(xbc, dt, h0, conv_w, conv_b, dt_bias, a_log, d_skip, *, rev, write_y, combine=None, y_dtype=F32):
    nb, t, _ = xbc.shape
    nc = t // SSM_CHUNK
    halo = 16
    hb = SSM_CHUNK // halo
    if rev:
        cidx = lambda k: nc - 1 - k
        hidx = lambda k: jnp.minimum((nc - k) * hb, t // halo - 1)
    else:
        cidx = lambda k: k
        hidx = lambda k: jnp.maximum(k * hb - 1, 0)
    chunk = lambda w: pl.BlockSpec((None, SSM_CHUNK, w), lambda b, k: (b, cidx(k), 0))
    state = pl.BlockSpec((None, SSM_GROUPS, SSM_STATE, SSM_GW), lambda b, k: (b, 0, 0, 0))
    in_specs = [chunk(SSM_CONV_CH),
                pl.BlockSpec((None, halo, SSM_CONV_CH), lambda b, k: (b, hidx(k), 0)),
                chunk(DT_PAD),
                _const_spec((SSM_CONV, SSM_CONV_CH)), _const_spec((1, SSM_CONV_CH)),
                _const_spec((1, DT_PAD)), _const_spec((1, DT_PAD)), _const_spec((1, D_INNER)), state]
    args = [xbc, xbc, dt, conv_w, conv_b, dt_bias, a_log, d_skip, h0]
    out_specs, out_shape = [], []
    if combine is not None:
        yf, z, ng = combine
        in_specs += [chunk(D_INNER), chunk(D_INNER), _const_spec((1, D_INNER))]
        args += [yf, z, ng]
    if write_y:
        out_specs.append(chunk(D_INNER))
        out_shape.append(jax.ShapeDtypeStruct((nb, t, D_INNER), y_dtype))
    out_specs.append(state)
    out_shape.append(jax.ShapeDtypeStruct((nb, SSM_GROUPS, SSM_STATE, SSM_GW), F32))
    outs = pl.pallas_call(
        functools.partial(_ssd_kernel, rev=rev, nc=nc, write_y=write_y, combine=combine is not None),
        grid=(nb, nc),
        in_specs=in_specs, out_specs=out_specs, out_shape=out_shape,
        scratch_shapes=[pltpu.VMEM((SSM_GROUPS, SSM_STATE, SSM_GW), F32),
                        pltpu.VMEM((SSM_CHUNK + halo, SSM_CONV_CH), F32)],
        compiler_params=_params(("parallel", "arbitrary")),
    )(*args)
    return outs if write_y else (None, outs[0])


def _resid_kernel(a_ref, w_ref, b_ref, x_ref, mod_ref, o_ref, *, gate_row):
    y = jnp.dot(a_ref[...], w_ref[...], preferred_element_type=F32) + b_ref[...]
    o_ref[...] = x_ref[...] + mod_ref[gate_row:gate_row + 1, :] * y


def _resid_matmul(a, w, bias, x, mod, layer, row, gate_row):
    nb, t, kdim = a.shape
    tm = _tile(t, TOKEN_TILE)
    tok = lambda wd: pl.BlockSpec((None, tm, wd), lambda b, i: (b, i, 0))
    return pl.pallas_call(
        functools.partial(_resid_kernel, gate_row=gate_row),
        grid=(nb, t // tm),
        in_specs=[tok(kdim), _const_spec((kdim, D)), _const_spec((1, D)), tok(D), _mod_spec(layer, row)],
        out_specs=tok(D),
        out_shape=jax.ShapeDtypeStruct((nb, t, D), F32),
        compiler_params=_params(("parallel", "parallel")),
    )(a, w, bias, x, mod)


def _mamba_layer(x, xc, mod, layer, nb, g1, p, need_ctx):
    wz, wx, wd, cw, cbias, dtb, alog, dsk, ng, wout = p
    z_l, xbc_l, dt_l = _mamba_in(x, mod, layer, None, g1, wz, wx, wd)
    z_c, xbc_c, dt_c = _mamba_in(xc, mod, layer, nb, g1, wz, wx, wd)
    h0 = jnp.zeros((x.shape[0], SSM_GROUPS, SSM_STATE, SSM_GW), F32)
    dirp = lambda d: (cw[d], cbias[d], dtb, alog, dsk[d])
    yf_c, hf = _ssd(xbc_c, dt_c, h0, *dirp(0), rev=False, write_y=need_ctx)
    yf_l, _ = _ssd(xbc_l, dt_l, hf, *dirp(0), rev=False, write_y=True)
    yn_c, hb = _ssd(xbc_c, dt_c, h0, *dirp(1), rev=True, write_y=need_ctx,
                    combine=(yf_c, z_c, ng) if need_ctx else None, y_dtype=BF16)
    yn_l, _ = _ssd(xbc_l, dt_l, hb, *dirp(1), rev=True, write_y=True, combine=(yf_l, z_l, ng), y_dtype=BF16)
    zero_b = jnp.zeros((1, D), F32)
    x = _resid_matmul(yn_l, wout, zero_b, x, mod, layer, None, 2)
    if need_ctx:
        xc = _resid_matmul(yn_c, wout, zero_b, xc, mod, layer, nb, 2)
    return x, xc


def _qkv_kernel(*refs, rope):
    if rope:
        x_ref, mod_ref, g_ref, wq_ref, wk_ref, wv_ref, cos_ref, sin_ref, q_ref, k_ref, v_ref = refs
    else:
        x_ref, mod_ref, g_ref, wq_ref, wk_ref, wv_ref, q_ref, k_ref, v_ref = refs
    gs = g_ref[...] * (1.0 + mod_ref[1:2, :])
    h = (_rms(x_ref[...]) * gs + mod_ref[0:1, :]).astype(BF16)
    tm = h.shape[0]
    if rope:
        cos, sin = cos_ref[...], sin_ref[...]
        lane = lax.broadcasted_iota(jnp.int32, (tm, 128), 1)
        first = (lane % 32) < 16

    def rot(a_ref, w_ref, scale):
        for j in range(D // 128):
            blk = jnp.dot(h, w_ref[:, j * 128:(j + 1) * 128], preferred_element_type=F32)
            if scale != 1.0:
                blk = blk * scale
            if rope:
                partner = jnp.where(first, pltpu.roll(blk, 112, 1), pltpu.roll(blk, 16, 1))
                blk = blk * cos + partner * sin
            a_ref[:, j * 128:(j + 1) * 128] = blk.astype(a_ref.dtype)

    rot(q_ref, wq_ref, DA_HEAD_DIM ** -0.5)
    rot(k_ref, wk_ref, 1.0)
    v_ref[...] = jnp.dot(h, wv_ref[...], preferred_element_type=F32).astype(v_ref.dtype)


def _rope_tables(s):
    lane = jnp.arange(128)
    pos = jnp.arange(s)
    coord = jnp.where(((lane % 64) // 32 == 0)[None, :], (pos // GRID_W)[:, None], (pos % GRID_W)[:, None])
    n_freq = DA_HEAD_DIM // 4
    inv = ROPE_THETA ** (-(lane % n_freq).astype(F32) / n_freq)
    ang = coord.astype(F32) * inv[None, :]
    sign = jnp.where((lane % 32) < 16, -1.0, 1.0)
    return jnp.cos(ang), jnp.sin(ang) * sign[None, :]


def _qkv(x, mod, layer, row, g, wq, wk, wv, tables):
    nb, t, _ = x.shape
    tm = _tile(t, TOKEN_TILE)
    tok = lambda: pl.BlockSpec((None, tm, D), lambda b, i: (b, i, 0))
    in_specs = [tok(), _mod_spec(layer, row), _const_spec((1, D))] + [_const_spec((D, D))] * 3
    args = [x, mod, g, wq, wk, wv]
    if tables is not None:
        in_specs += [pl.BlockSpec((tm, 128), lambda b, i: (i, 0))] * 2
        args += list(tables)
    return pl.pallas_call(
        functools.partial(_qkv_kernel, rope=tables is not None),
        grid=(nb, t // tm),
        in_specs=in_specs, out_specs=[tok()] * 3,
        out_shape=[jax.ShapeDtypeStruct((nb, t, D), BF16)] * 3,
        compiler_params=_params(("parallel", "parallel")),
    )(*args)


def _attn_kernel(*refs, nseg, lam_init):
    lam_ref, ng_ref, q_ref = refs[:3]
    kv = refs[3:3 + 2 * nseg]
    o_ref = refs[3 + 2 * nseg]
    q = q_ref[...]
    tq = q.shape[0]
    lane = lax.broadcasted_iota(jnp.int32, q.shape, 1)
    zero = jnp.zeros_like(q)
    qs = [jnp.where((lane >= e * DA_HEAD_DIM) & (lane < (e + 1) * DA_HEAD_DIM), q, zero) for e in range(2)]
    scores = [[lax.dot_general(qe, kv[2 * i][...], (((1,), (1,)), ((), ())), preferred_element_type=F32)
               for i in range(nseg)] for qe in qs]
    halves = []
    for e in range(2):
        m = scores[e][0].max(axis=-1, keepdims=True)
        for s in scores[e][1:]:
            m = jnp.maximum(m, s.max(axis=-1, keepdims=True))
        den = jnp.zeros_like(m)
        acc = jnp.zeros((tq, DA_V), F32)
        for i in range(nseg):
            p = jnp.exp(scores[e][i] - m)
            den = den + p.sum(axis=-1, keepdims=True)
            acc = acc + jnp.dot(p.astype(BF16), kv[2 * i + 1][...], preferred_element_type=F32)
        halves.append(acc / den)
    lp = lam_ref[...]
    lam = (jnp.exp(jnp.sum(lp[0:1] * lp[1:2], axis=-1, keepdims=True))
           - jnp.exp(jnp.sum(lp[2:3] * lp[3:4], axis=-1, keepdims=True)) + lam_init)
    o = halves[0] - lam * halves[1]
    o_ref[...] = (_rms(o) * ng_ref[...] * (1.0 - lam_init)).astype(o_ref.dtype)


def _attention(q, segs, lam_p, ng, lam_init):
    nb, t, _ = q.shape
    tq = _tile(t, ATTN_Q_TILE)
    in_specs = [_const_spec((4, DA_HEAD_DIM)), _const_spec((1, DA_V)),
                pl.BlockSpec((None, tq, DA_V), lambda b, h, i: (b, i, h))]
    args = [lam_p, ng, q]
    for k, v in segs:
        tk = k.shape[1]
        in_specs += [pl.BlockSpec((None, tk, DA_V), lambda b, h, i: (b, 0, h))] * 2
        args += [k, v]
    return pl.pallas_call(
        functools.partial(_attn_kernel, nseg=len(segs), lam_init=lam_init),
        grid=(nb, DA_HEADS, t // tq),
        in_specs=in_specs,
        out_specs=pl.BlockSpec((None, tq, DA_V), lambda b, h, i: (b, i, h)),
        out_shape=jax.ShapeDtypeStruct((nb, t, D), BF16),
        compiler_params=_params(("parallel", "parallel", "arbitrary")),
    )(*args)


def _attn_layer(x, xc, mod, layer, nb, g1, p, need_ctx):
    wq, wk, wv, lam_p, ng, wout = p
    lam_init = 0.8 - 0.6 * math.exp(-0.3 * layer)
    q_l, k_l, v_l = _qkv(x, mod, layer, None, g1, wq, wk, wv, _rope_tables(x.shape[1]))
    q_c, k_c, v_c = _qkv(xc, mod, layer, nb, g1, wq, wk, wv, None)
    zero_b = jnp.zeros((1, D), F32)
    o_l = _attention(q_l, [(k_c, v_c), (k_l, v_l)], lam_p, ng, lam_init)
    x = _resid_matmul(o_l, wout, zero_b, x, mod, layer, None, 2)
    if need_ctx:
        o_c = _attention(q_c, [(k_c, v_c)], lam_p, ng, lam_init)
        xc = _resid_matmul(o_c, wout, zero_b, xc, mod, layer, nb, 2)
    return x, xc


def _conf_kernel(xm_ref, xp_ref, xn_ref, mod_ref, g_ref, w1_ref, b1_ref, dw_ref, db_ref, lg_ref, lb_ref,
                 w2_ref, b2_ref, o_ref, u_ref, c_ref, *, tm, nt):
    t = pl.program_id(1)
    hl = CONF_HALO
    gs = g_ref[...] * (1.0 + mod_ref[1:2, :])
    shift = mod_ref[0:1, :]

    def glu(x, valid):
        h = (_rms(x) * gs + shift).astype(BF16)
        a = jnp.dot(h, w1_ref[...], preferred_element_type=F32) + b1_ref[...]
        u = a[:, :D] * _sigmoid(a[:, D:])
        return u if valid is None else jnp.where(valid, u, 0.0)

    u_ref[0:hl, :] = glu(xp_ref[...], t > 0)
    u_ref[hl:hl + tm, :] = glu(xm_ref[...], None)
    u_ref[hl + tm:, :] = glu(xn_ref[...], t < nt - 1)

    rb, cw = 64, 256
    pad = CONF_KERNEL // 2
    for r0 in range(0, tm, rb):
        for c0 in range(0, D, cw):
            acc = jnp.broadcast_to(db_ref[:, c0:c0 + cw], (rb, cw))
            for kk in range(CONF_KERNEL):
                s = hl - pad + kk + r0
                acc = acc + u_ref[s:s + rb, c0:c0 + cw] * dw_ref[kk:kk + 1, c0:c0 + cw]
            c_ref[r0:r0 + rb, c0:c0 + cw] = acc

    cv = c_ref[...]
    mu = jnp.mean(cv, axis=-1, keepdims=True)
    xc = cv - mu
    ln = xc * lax.rsqrt(jnp.mean(xc * xc, axis=-1, keepdims=True) + EPS) * lg_ref[...] + lb_ref[...]
    y = jnp.dot(_silu(ln).astype(BF16), w2_ref[...], preferred_element_type=F32) + b2_ref[...]
    o_ref[...] = xm_ref[...] + mod_ref[2:3, :] * y


def _conformer(x, mod, layer, row, g, p):
    w1, b1, dw, db, lg, lb, w2, b2 = p
    nb, t, _ = x.shape
    tm = _tile(t, 256)
    nt = t // tm
    hl = CONF_HALO
    r = tm // hl
    return pl.pallas_call(
        functools.partial(_conf_kernel, tm=tm, nt=nt),
        grid=(nb, nt),
        in_specs=[pl.BlockSpec((None, tm, D), lambda b, i: (b, i, 0)),
                  pl.BlockSpec((None, hl, D), lambda b, i: (b, jnp.maximum(i * r - 1, 0), 0)),
                  pl.BlockSpec((None, hl, D), lambda b, i: (b, jnp.minimum((i + 1) * r, t // hl - 1), 0)),
                  _mod_spec(layer, row), _const_spec((1, D)),
                  _const_spec((D, 2 * D)), _const_spec((1, 2 * D)),
                  _const_spec((CONF_KERNEL, D)), _const_spec((1, D)), _const_spec((1, D)), _const_spec((1, D)),
                  _const_spec((D, D)), _const_spec((1, D))],
        out_specs=pl.BlockSpec((None, tm, D), lambda b, i: (b, i, 0)),
        out_shape=jax.ShapeDtypeStruct((nb, t, D), F32),
        scratch_shapes=[pltpu.VMEM((tm + 2 * hl, D), F32), pltpu.VMEM((tm, D), F32)],
        compiler_params=_params(("parallel", "parallel")),
    )(x, x, x, mod, g, w1, b1, dw, db, lg, lb, w2, b2)


def _ffn_kernel(*refs, tm, nt, final):
    if final:
        (xm_ref, xp_ref, xn_ref, mod_ref, g_ref, wu_ref, cw_ref, cb_ref, wd_ref, fg_ref,
         o_ref, h_ref, u_ref, act_ref) = refs
    else:
        (xm_ref, xp_ref, xn_ref, mod_ref, g_ref, wu_ref, cw_ref, cb_ref, wd_ref,
         o_ref, h_ref, u_ref, act_ref) = refs
    t = pl.program_id(1)
    hl = FFN_HALO
    gs = g_ref[...] * (1.0 + mod_ref[4:5, :])
    shift = mod_ref[3:4, :]
    mod = lambda x: _rms(x) * gs + shift
    hp = jnp.where(t > 0, mod(xp_ref[...]), 0.0)
    hn = jnp.where(t < nt - 1, mod(xn_ref[...]), 0.0)
    h_ref[...] = jnp.concatenate([hp, mod(xm_ref[...]), hn], axis=0).astype(BF16)

    for j in range(FFN_NCHUNK):
        ub = u_ref.at[j % 2]
        ub[...] = jnp.dot(h_ref[...], wu_ref[j], preferred_element_type=F32)
        w = cw_ref[j]
        cv = (cb_ref[j] + w[0:1, :] * ub[hl - 1:hl - 1 + tm, :] + w[1:2, :] * ub[hl:hl + tm, :]
              + w[2:3, :] * ub[hl + 1:hl + 1 + tm, :])
        act_ref[:, j * FFN_CHUNK:(j + 1) * FFN_CHUNK] = (_silu(cv[:, :FFN_CHUNK]) * cv[:, FFN_CHUNK:]).astype(BF16)

    y = jnp.dot(act_ref[...], wd_ref[...], preferred_element_type=F32)
    out = xm_ref[...] + mod_ref[5:6, :] * y
    if final:
        out = _rms(out) * fg_ref[...]
    o_ref[...] = out


def _ffn(x, mod, layer, row, g, p, final_g=None):
    wu, cw, cb, wd = p
    nb, t, _ = x.shape
    tm = _tile(t, TOKEN_TILE)
    nt = t // tm
    hl = FFN_HALO
    r = tm // hl
    final = final_g is not None
    in_specs = [pl.BlockSpec((None, tm, D), lambda b, i: (b, i, 0)),
                pl.BlockSpec((None, hl, D), lambda b, i: (b, jnp.maximum(i * r - 1, 0), 0)),
                pl.BlockSpec((None, hl, D), lambda b, i: (b, jnp.minimum((i + 1) * r, t // hl - 1), 0)),
                _mod_spec(layer, row), _const_spec((1, D)),
                _const_spec((FFN_NCHUNK, D, 2 * FFN_CHUNK)), _const_spec((FFN_NCHUNK, 3, 2 * FFN_CHUNK)),
                _const_spec((FFN_NCHUNK, 1, 2 * FFN_CHUNK)), _const_spec((FFN_HIDDEN, D))]
    args = [x, x, x, mod, g, wu, cw, cb, wd]
    if final:
        in_specs.append(_const_spec((1, D)))
        args.append(final_g)
    return pl.pallas_call(
        functools.partial(_ffn_kernel, tm=tm, nt=nt, final=final),
        grid=(nb, nt),
        in_specs=in_specs,
        out_specs=pl.BlockSpec((None, tm, D), lambda b, i: (b, i, 0)),
        out_shape=jax.ShapeDtypeStruct((nb, t, D), F32),
        scratch_shapes=[pltpu.VMEM((tm + 2 * hl, D), BF16), pltpu.VMEM((2, tm + 2 * hl, 2 * FFN_CHUNK), F32),
                        pltpu.VMEM((tm, FFN_HIDDEN), BF16)],
        compiler_params=_params(("parallel", "parallel")),
    )(*args)


def _ffn_weights(w_up, conv_w, conv_b, w_down):
    def pair(a):
        lead = a.shape[:-1]
        a = a.reshape(lead + (2, FFN_NCHUNK, FFN_CHUNK))
        a = jnp.moveaxis(a, -2, 0)
        return a.reshape((FFN_NCHUNK,) + lead + (2 * FFN_CHUNK,))
    return (pair(w_up).astype(BF16), pair(conv_w), pair(conv_b[None, :]),
            w_down.astype(BF16))


def kernel(x, c, ctx, c_ctx, mod_w, mod_b, norm1_g, norm2_g, ffn_w_up, ffn_conv_w, ffn_conv_b, ffn_w_down,
           ssm_w_in, ssm_conv_w, ssm_conv_b, ssm_dt_bias, ssm_a_log, ssm_d, ssm_norm_g, ssm_w_out, attn_w_in,
           attn_lambda, attn_norm_g, attn_w_out, conf_w_pw1, conf_b_pw1, conf_dw_w, conf_dw_b, conf_ln_g,
           conf_ln_b, conf_w_pw2, conf_b_pw2, final_g):
    depth = mod_w.shape[0]
    nb = x.shape[0]
    mod = _mod_table(c, c_ctx, mod_w, mod_b)
    xc = ctx
    row = lambda a: a.reshape(1, -1)
    pad_dt = lambda a: jnp.pad(a.reshape(1, -1), ((0, 0), (0, DT_PAD - 2 * SSM_HEADS)))
    for i in range(depth):
        kind, j = i % N_MIXERS, i // N_MIXERS
        need_ctx = i < depth - 1
        g1 = row(norm1_g[i])
        if kind == 0:
            w_in = ssm_w_in[j]
            o_x, o_dt = D_INNER, D_INNER + SSM_CONV_CH
            wd = jnp.pad(w_in[:, o_dt:], ((0, 0), (0, DT_PAD - 2 * SSM_HEADS)))
            p = (w_in[:, :o_x].astype(BF16), w_in[:, o_x:o_dt].astype(BF16), wd.astype(BF16),
                 ssm_conv_w[j], ssm_conv_b[j][:, None, :], pad_dt(ssm_dt_bias[j]), pad_dt(ssm_a_log[j]),
                 jnp.repeat(ssm_d[j], SSM_HEAD_DIM, axis=-1)[:, None, :], row(ssm_norm_g[j]),
                 ssm_w_out[j].astype(BF16))
            x, xc = _mamba_layer(x, xc, mod, i, nb, g1, p, need_ctx)
        elif kind == 1:
            w_in = attn_w_in[j].astype(BF16)
            p = (w_in[:, :D], w_in[:, D:2 * D], w_in[:, 2 * D:], attn_lambda[j], row(attn_norm_g[j]),
                 attn_w_out[j].astype(BF16))
            x, xc = _attn_layer(x, xc, mod, i, nb, g1, p, need_ctx)
        else:
            p = (conf_w_pw1[j].astype(BF16), row(conf_b_pw1[j]), conf_dw_w[j], row(conf_dw_b[j]),
                 row(conf_ln_g[j]), row(conf_ln_b[j]), conf_w_pw2[j].astype(BF16), row(conf_b_pw2[j]))
            x = _conformer(x, mod, i, None, g1, p)
            if need_ctx:
                xc = _conformer(xc, mod, i, nb, g1, p)
        fp = _ffn_weights(ffn_w_up[i], ffn_conv_w[i], ffn_conv_b[i], ffn_w_down[i])
        g2 = row(norm2_g[i])
        x = _ffn(x, mod, i, None, g2, fp, final_g=row(final_g) if i == depth - 1 else None)
        if need_ctx:
            xc = _ffn(xc, mod, i, nb, g2, fp)
    return x
```

```python
import functools
import math

import jax
import jax.numpy as jnp
from jax import lax
from jax.experimental import pallas as pl
from jax.experimental.pallas import tpu as pltpu

F32 = jnp.float32
BF16 = jnp.bfloat16
HIGHEST = lax.Precision.HIGHEST

D = 1024
N_MOD = 6
N_MIXERS = 3
EPS = 1e-6
GRID_W = 64
ROPE_THETA = 10000.0

SSM_HEAD_DIM = 64
SSM_HEADS = 32
SSM_GROUPS = 8
SSM_HPG = SSM_HEADS // SSM_GROUPS
SSM_STATE = 128
SSM_CONV = 4
SSM_CHUNK = 128
D_INNER = SSM_HEADS * SSM_HEAD_DIM
SSM_GW = SSM_HPG * SSM_HEAD_DIM
SSM_BC = SSM_GROUPS * SSM_STATE
SSM_CONV_CH = D_INNER + 2 * SSM_BC
DT_PAD = 128
MAMBA_HALO = 8
MAMBA_COLS = 512
SSD_HALO = 16

DA_HEAD_DIM = 64
DA_HEADS = D // (2 * DA_HEAD_DIM)
DA_V = 2 * DA_HEAD_DIM

CONF_KERNEL = 31
CONF_HALO = 16

FFN_HIDDEN = 2816
FFN_CHUNK = 256
FFN_NCHUNK = FFN_HIDDEN // FFN_CHUNK
FFN_HALO = 8

VMEM_LIMIT = 56 * 1024 * 1024
TOKEN_TILE = 512
ATTN_Q_TILE = 256


def _sigmoid(x):
    return 1.0 / (1.0 + jnp.exp(-x))


def _silu(x):
    return x * _sigmoid(x)


def _softplus(x):
    return jnp.maximum(x, 0.0) + jnp.log(1.0 + jnp.exp(-jnp.abs(x)))


def _rms(x):
    return x * lax.rsqrt(jnp.mean(x * x, axis=-1, keepdims=True) + EPS)


def _params(sem, vmem=VMEM_LIMIT):
    return pltpu.CompilerParams(dimension_semantics=sem, vmem_limit_bytes=vmem)


def _mod_spec(layer, row):
    if row is None:
        return pl.BlockSpec((None, None, N_MOD, D), lambda b, *_: (layer, b, 0, 0))
    return pl.BlockSpec((None, None, N_MOD, D), lambda b, *_: (layer, row, 0, 0))


def _const_spec(shape):
    nd = len(shape)
    return pl.BlockSpec(shape, lambda *_: (0,) * nd)


def _tile(n, pref):
    return pref if n % pref == 0 else n


def _mod_kernel(s_ref, w_ref, b_ref, o_ref):
    s = _silu(s_ref[...])
    o_ref[...] = jnp.dot(s, w_ref[...], preferred_element_type=F32, precision=HIGHEST) + b_ref[...]


def _mod_table(c, c_ctx, mod_w, mod_b):
    depth = mod_w.shape[0]
    nb = c.shape[0]
    rows = -(-(nb + 1) // 8) * 8
    s = jnp.zeros((rows, D), F32).at[:nb].set(c).at[nb].set(c_ctx)
    tn = 1536
    out = pl.pallas_call(
        _mod_kernel,
        grid=(depth, N_MOD * D // tn),
        in_specs=[pl.BlockSpec((rows, D), lambda i, n: (0, 0)),
                  pl.BlockSpec((None, D, tn), lambda i, n: (i, 0, n)),
                  pl.BlockSpec((None, 1, tn), lambda i, n: (i, 0, n))],
        out_specs=pl.BlockSpec((None, rows, tn), lambda i, n: (i, 0, n)),
        out_shape=jax.ShapeDtypeStruct((depth, rows, N_MOD * D), F32),
        compiler_params=_params(("arbitrary", "arbitrary")),
    )(s, mod_w, mod_b.reshape(depth, 1, N_MOD * D))
    return out.reshape(depth, rows, N_MOD, D)


def _mamba_in_kernel(xm_ref, xp_ref, xn_ref, mod_ref, g_ref, wz_ref, wx_ref, wd_ref, cw_ref, cb_ref,
                     z_ref, uf_ref, xbc_ref, dt_ref, he_ref, hm_ref, xe_ref, *, tm, nt):
    t = pl.program_id(1)
    hl = MAMBA_HALO
    gs = g_ref[...] * (1.0 + mod_ref[1:2, :])
    shift = mod_ref[0:1, :]
    mod = lambda x: _rms(x) * gs + shift
    hm = mod(xm_ref[...])
    hp = jnp.where(t > 0, mod(xp_ref[...]), 0.0)
    hn = jnp.where(t < nt - 1, mod(xn_ref[...]), 0.0)
    he_ref[...] = jnp.concatenate([hp, hm, hn], axis=0).astype(BF16)
    hm_ref[...] = hm.astype(BF16)
    dt_ref[...] = jnp.dot(hm_ref[...], wd_ref[...], preferred_element_type=F32)
    nt8, nm8 = (tm + 2 * hl) // 8, tm // 8
    sub = lax.broadcasted_iota(jnp.int32, (nm8, 8, MAMBA_COLS), 1)
    zw = D_INNER * MAMBA_COLS // SSM_CONV_CH
    for c0 in range(0, SSM_CONV_CH, MAMBA_COLS):
        cols = slice(c0, c0 + MAMBA_COLS)
        zc = slice(c0 // MAMBA_COLS * zw, (c0 // MAMBA_COLS + 1) * zw)
        z_ref[:, zc] = jnp.dot(hm_ref[...], wz_ref[:, zc], preferred_element_type=F32).astype(z_ref.dtype)
        xb = xe_ref.at[(c0 // MAMBA_COLS) % 2]
        xe = jnp.dot(he_ref[...], wx_ref[:, cols], preferred_element_type=F32)
        xb[...] = xe.reshape(nt8, 8, MAMBA_COLS)
        xbc_ref[:, cols] = xb[1:1 + nm8].reshape(tm, MAMBA_COLS).astype(xbc_ref.dtype)
        acc = cb_ref[:, cols] + xb[1:1 + nm8] * cw_ref[SSM_CONV - 1:SSM_CONV, cols]
        for k in range(1, SSM_CONV):
            rk = pltpu.roll(xb[...], k, 1)
            acc = acc + jnp.where(sub >= k, rk[1:1 + nm8], rk[0:nm8]) * cw_ref[SSM_CONV - 1 - k:SSM_CONV - k, cols]
        uf_ref[:, cols] = _silu(acc).reshape(tm, MAMBA_COLS).astype(uf_ref.dtype)


def _mamba_in(x, mod, layer, row, g, wz, wx, wd, cw, cb):
    nb, t, _ = x.shape
    tm = _tile(t, TOKEN_TILE)
    nt = t // tm
    hl = MAMBA_HALO
    r = tm // hl
    tok = lambda w: pl.BlockSpec((None, tm, w), lambda b, i: (b, i, 0))
    return pl.pallas_call(
        functools.partial(_mamba_in_kernel, tm=tm, nt=nt),
        grid=(nb, nt),
        in_specs=[tok(D),
                  pl.BlockSpec((None, hl, D), lambda b, i: (b, jnp.maximum(i * r - 1, 0), 0)),
                  pl.BlockSpec((None, hl, D), lambda b, i: (b, jnp.minimum((i + 1) * r, t // hl - 1), 0)),
                  _mod_spec(layer, row), _const_spec((1, D)),
                  _const_spec((D, D_INNER)), _const_spec((D, SSM_CONV_CH)), _const_spec((D, DT_PAD)),
                  _const_spec((SSM_CONV, SSM_CONV_CH)), _const_spec((1, SSM_CONV_CH))],
        out_specs=[tok(D_INNER), tok(SSM_CONV_CH), tok(SSM_CONV_CH), tok(DT_PAD)],
        out_shape=[jax.ShapeDtypeStruct((nb, t, D_INNER), BF16),
                   jax.ShapeDtypeStruct((nb, t, SSM_CONV_CH), BF16),
                   jax.ShapeDtypeStruct((nb, t, SSM_CONV_CH), BF16),
                   jax.ShapeDtypeStruct((nb, t, DT_PAD), F32)],
        scratch_shapes=[pltpu.VMEM((tm + 2 * hl, D), BF16), pltpu.VMEM((tm, D), BF16),
                        pltpu.VMEM((2, (tm + 2 * hl) // 8, 8, MAMBA_COLS), F32)],
        compiler_params=_params(("parallel", "parallel")),
    )(x, x, x, mod, g, wz, wx, wd, cw, cb)


def _ssd_kernel(*refs, rev, nc, write_y, combine, conv):
    it = iter(refs)
    u_ref = next(it)
    if conv:
        xh_ref, cw_ref, cbias_ref = (next(it) for _ in range(3))
    dt_ref, dtb_ref, alog_ref, dsk_ref, h0_ref = (next(it) for _ in range(5))
    if combine:
        yf_ref, z_ref, ng_ref = (next(it) for _ in range(3))
    y_ref = next(it) if write_y else None
    hfin_ref = next(it)
    ht_ref = next(it)

    k = pl.program_id(1)
    ch = SSM_CHUNK
    off = SSM_HEADS if rev else 0

    @pl.when(k == 0)
    def _():
        ht_ref[...] = h0_ref[...]

    if conv:
        xbuf_ref = next(it)
        c = (nc - 1 - k) if rev else k
        main = u_ref[...].astype(F32)
        side = xh_ref[...].astype(F32)
        if rev:
            xbuf_ref[0:ch, :] = main
            xbuf_ref[ch:ch + SSD_HALO, :] = jnp.where(c < nc - 1, side, 0.0)
            starts = [SSM_CONV - 1 - kk for kk in range(SSM_CONV)]
        else:
            xbuf_ref[0:SSD_HALO, :] = jnp.where(c > 0, side, 0.0)
            xbuf_ref[SSD_HALO:SSD_HALO + ch, :] = main
            starts = [SSD_HALO - (SSM_CONV - 1) + kk for kk in range(SSM_CONV)]

        def load(c0, w):
            acc = cbias_ref[:, c0:c0 + w]
            for kk in range(SSM_CONV):
                acc = acc + xbuf_ref[starts[kk]:starts[kk] + ch, c0:c0 + w] * cw_ref[kk:kk + 1, c0:c0 + w]
            return _silu(acc)
    else:
        def load(c0, w):
            return u_ref[:, c0:c0 + w]

    dtv = _softplus(dt_ref[...] + dtb_ref[...])
    la = dtv * (-jnp.exp(alog_ref[...]))
    ri = lax.broadcasted_iota(jnp.int32, (ch, ch), 0)
    ci = lax.broadcasted_iota(jnp.int32, (ch, ch), 1)
    keep = (ci >= ri) if rev else (ci <= ri)
    acum = jnp.dot(keep.astype(F32), la, preferred_element_type=F32, precision=HIGHEST)
    last = 0 if rev else ch - 1
    total = acum[last:last + 1, :]
    wcol = jnp.exp(total - acum) * dtv
    offv = jnp.exp(acum)
    acum_t = acum.T
    dt_t = dtv.T

    lane = lax.broadcasted_iota(jnp.int32, (ch, SSM_GW), 1)

    def expand(colmat, g):
        out = jnp.broadcast_to(colmat[:, off + SSM_HPG * g:off + SSM_HPG * g + 1], (ch, SSM_GW))
        for j in range(1, SSM_HPG):
            col = off + SSM_HPG * g + j
            out = jnp.where(lane >= j * SSM_HEAD_DIM, jnp.broadcast_to(colmat[:, col:col + 1], (ch, SSM_GW)), out)
        return out

    for g in range(SSM_GROUPS):
        xg = load(g * SSM_GW, SSM_GW)
        xg16, xg = xg.astype(BF16), xg.astype(F32)
        bg16 = load(D_INNER + g * SSM_STATE, SSM_STATE).astype(BF16)
        xw = (xg * expand(wcol, g)).astype(BF16)
        st_t = lax.dot_general(bg16, xw, (((0,), (0,)), ((), ())), preferred_element_type=F32)
        sg = expand(offv, g)
        ht_old = ht_ref[g]
        if write_y:
            cg16 = load(D_INNER + SSM_BC + g * SSM_STATE, SSM_STATE).astype(BF16)
            cbm = lax.dot_general(cg16, bg16, (((1,), (1,)), ((), ())), preferred_element_type=F32)
            yoff = jnp.dot(cg16, ht_old.astype(BF16), preferred_element_type=F32)
            ms, xs = [], []
            for j in range(SSM_HPG):
                col = off + SSM_HPG * g + j
                seg = acum[:, col:col + 1] - acum_t[col:col + 1, :]
                lmat = jnp.exp(jnp.where(keep, seg, -jnp.inf))
                ms.append((cbm * lmat * dt_t[col:col + 1, :]).astype(BF16))
                in_head = (lane >= j * SSM_HEAD_DIM) & (lane < (j + 1) * SSM_HEAD_DIM)
                xs.append(jnp.where(in_head, xg16, jnp.zeros_like(xg16)))
            ydiag = jnp.dot(jnp.concatenate(ms, axis=1), jnp.concatenate(xs, axis=0),
                            preferred_element_type=F32)
            y = ydiag + yoff * sg + dsk_ref[:, g * SSM_GW:(g + 1) * SSM_GW] * xg
            cols = slice(g * SSM_GW, (g + 1) * SSM_GW)
            if combine:
                y = (y + yf_ref[:, cols].astype(F32)) * _silu(z_ref[:, cols].astype(F32))
                y = _rms(y) * ng_ref[:, cols]
            y_ref[:, cols] = y.astype(y_ref.dtype)
        ht_ref[g] = ht_old * sg[last:last + 1, :] + st_t

    @pl.when(k == nc - 1)
    def _():
        hfin_ref[...] = ht_ref[...]


def _ssd(u, dt, h0, dt_bias, a_log, d_skip, *, rev, write_y, taps=None, combine=None, y_dtype=F32):
    nb, t, _ = u.shape
    nc = t // SSM_CHUNK
    hb = SSM_CHUNK // SSD_HALO
    if rev:
        cidx = lambda k: nc - 1 - k
        hidx = lambda k: jnp.minimum((nc - k) * hb, t // SSD_HALO - 1)
    else:
        cidx = lambda k: k
        hidx = lambda k: jnp.maximum(k * hb - 1, 0)
    chunk = lambda w: pl.BlockSpec((None, SSM_CHUNK, w), lambda b, k: (b, cidx(k), 0))
    state = pl.BlockSpec((None, SSM_GROUPS, SSM_STATE, SSM_GW), lambda b, k: (b, 0, 0, 0))
    in_specs, args = [chunk(SSM_CONV_CH)], [u]
    scratch = [pltpu.VMEM((SSM_GROUPS, SSM_STATE, SSM_GW), F32)]
    if taps is not None:
        in_specs += [pl.BlockSpec((None, SSD_HALO, SSM_CONV_CH), lambda b, k: (b, hidx(k), 0)),
                     _const_spec((SSM_CONV, SSM_CONV_CH)), _const_spec((1, SSM_CONV_CH))]
        args += [u, taps[0], taps[1]]
        scratch.append(pltpu.VMEM((SSM_CHUNK + SSD_HALO, SSM_CONV_CH), F32))
    in_specs += [chunk(DT_PAD), _const_spec((1, DT_PAD)), _const_spec((1, DT_PAD)), _const_spec((1, D_INNER)), state]
    args += [dt, dt_bias, a_log, d_skip, h0]
    out_specs, out_shape = [], []
    if combine is not None:
        yf, z, ng = combine
        in_specs += [chunk(D_INNER), chunk(D_INNER), _const_spec((1, D_INNER))]
        args += [yf, z, ng]
    if write_y:
        out_specs.append(chunk(D_INNER))
        out_shape.append(jax.ShapeDtypeStruct((nb, t, D_INNER), y_dtype))
    out_specs.append(state)
    out_shape.append(jax.ShapeDtypeStruct((nb, SSM_GROUPS, SSM_STATE, SSM_GW), F32))
    outs = pl.pallas_call(
        functools.partial(_ssd_kernel, rev=rev, nc=nc, write_y=write_y, combine=combine is not None,
                          conv=taps is not None),
        grid=(nb, nc),
        in_specs=in_specs, out_specs=out_specs, out_shape=out_shape,
        scratch_shapes=scratch,
        compiler_params=_params(("parallel", "arbitrary")),
    )(*args)
    return outs if write_y else (None, outs[0])


def _resid_kernel(a_ref, w_ref, b_ref, x_ref, mod_ref, o_ref, *, gate_row):
    y = jnp.dot(a_ref[...], w_ref[...], preferred_element_type=F32) + b_ref[...]
    o_ref[...] = x_ref[...] + mod_ref[gate_row:gate_row + 1, :] * y


def _resid_matmul(a, w, bias, x, mod, layer, row, gate_row):
    nb, t, kdim = a.shape
    tm = _tile(t, TOKEN_TILE)
    tok = lambda wd: pl.BlockSpec((None, tm, wd), lambda b, i: (b, i, 0))
    return pl.pallas_call(
        functools.partial(_resid_kernel, gate_row=gate_row),
        grid=(nb, t // tm),
        in_specs=[tok(kdim), _const_spec((kdim, D)), _const_spec((1, D)), tok(D), _mod_spec(layer, row)],
        out_specs=tok(D),
        out_shape=jax.ShapeDtypeStruct((nb, t, D), F32),
        compiler_params=_params(("parallel", "parallel")),
    )(a, w, bias, x, mod)


def _mamba_layer(x, xc, mod, layer, nb, g1, p, need_ctx):
    wz, wx, wd, cw, cbias, dtb, alog, dsk, ng, wout = p
    z_l, uf_l, xbc_l, dt_l = _mamba_in(x, mod, layer, None, g1, wz, wx, wd, cw[0], cbias[0])
    z_c, uf_c, xbc_c, dt_c = _mamba_in(xc, mod, layer, nb, g1, wz, wx, wd, cw[0], cbias[0])
    h0 = jnp.zeros((x.shape[0], SSM_GROUPS, SSM_STATE, SSM_GW), F32)
    dirp = lambda d: (dtb, alog, dsk[d])
    yf_c, hf = _ssd(uf_c, dt_c, h0, *dirp(0), rev=False, write_y=need_ctx)
    yf_l, _ = _ssd(uf_l, dt_l, hf, *dirp(0), rev=False, write_y=True)
    taps = (cw[1], cbias[1])
    yn_c, hb = _ssd(xbc_c, dt_c, h0, *dirp(1), rev=True, write_y=need_ctx, taps=taps,
                    combine=(yf_c, z_c, ng) if need_ctx else None, y_dtype=BF16)
    yn_l, _ = _ssd(xbc_l, dt_l, hb, *dirp(1), rev=True, write_y=True, taps=taps, combine=(yf_l, z_l, ng),
                   y_dtype=BF16)
    zero_b = jnp.zeros((1, D), F32)
    x = _resid_matmul(yn_l, wout, zero_b, x, mod, layer, None, 2)
    if need_ctx:
        xc = _resid_matmul(yn_c, wout, zero_b, xc, mod, layer, nb, 2)
    return x, xc


def _qkv_kernel(*refs, rope):
    if rope:
        x_ref, mod_ref, g_ref, wq_ref, wk_ref, wv_ref, cos_ref, sin_ref, q_ref, k_ref, v_ref = refs
    else:
        x_ref, mod_ref, g_ref, wq_ref, wk_ref, wv_ref, q_ref, k_ref, v_ref = refs
    gs = g_ref[...] * (1.0 + mod_ref[1:2, :])
    h = (_rms(x_ref[...]) * gs + mod_ref[0:1, :]).astype(BF16)
    tm = h.shape[0]
    if rope:
        cos, sin = cos_ref[...], sin_ref[...]
        lane = lax.broadcasted_iota(jnp.int32, (tm, 128), 1)
        first = (lane % 32) < 16

    def rot(a_ref, w_ref, scale):
        for j in range(D // 128):
            blk = jnp.dot(h, w_ref[:, j * 128:(j + 1) * 128], preferred_element_type=F32)
            if scale != 1.0:
                blk = blk * scale
            if rope:
                partner = jnp.where(first, pltpu.roll(blk, 112, 1), pltpu.roll(blk, 16, 1))
                blk = blk * cos + partner * sin
            a_ref[:, j * 128:(j + 1) * 128] = blk.astype(a_ref.dtype)

    rot(q_ref, wq_ref, DA_HEAD_DIM ** -0.5 * math.log2(math.e))
    rot(k_ref, wk_ref, 1.0)
    v_ref[...] = jnp.dot(h, wv_ref[...], preferred_element_type=F32).astype(v_ref.dtype)


def _rope_tables(s):
    lane = jnp.arange(128)
    pos = jnp.arange(s)
    coord = jnp.where(((lane % 64) // 32 == 0)[None, :], (pos // GRID_W)[:, None], (pos % GRID_W)[:, None])
    n_freq = DA_HEAD_DIM // 4
    inv = ROPE_THETA ** (-(lane % n_freq).astype(F32) / n_freq)
    ang = coord.astype(F32) * inv[None, :]
    sign = jnp.where((lane % 32) < 16, -1.0, 1.0)
    return jnp.cos(ang), jnp.sin(ang) * sign[None, :]


def _qkv(x, mod, layer, row, g, wq, wk, wv, tables):
    nb, t, _ = x.shape
    tm = _tile(t, TOKEN_TILE)
    tok = lambda: pl.BlockSpec((None, tm, D), lambda b, i: (b, i, 0))
    in_specs = [tok(), _mod_spec(layer, row), _const_spec((1, D))] + [_const_spec((D, D))] * 3
    args = [x, mod, g, wq, wk, wv]
    if tables is not None:
        in_specs += [pl.BlockSpec((tm, 128), lambda b, i: (i, 0))] * 2
        args += list(tables)
    return pl.pallas_call(
        functools.partial(_qkv_kernel, rope=tables is not None),
        grid=(nb, t // tm),
        in_specs=in_specs, out_specs=[tok()] * 3,
        out_shape=[jax.ShapeDtypeStruct((nb, t, D), BF16)] * 3,
        compiler_params=_params(("parallel", "parallel")),
    )(*args)


def _attn_kernel(*refs, nseg, lam_init):
    lam_ref, ng_ref, q_ref = refs[:3]
    kv = refs[3:3 + 2 * nseg]
    o_ref = refs[3 + 2 * nseg]
    q = q_ref[...]
    tq = q.shape[0]
    lane = lax.broadcasted_iota(jnp.int32, q.shape, 1)
    zero = jnp.zeros_like(q)
    qs = [jnp.where((lane >= e * DA_HEAD_DIM) & (lane < (e + 1) * DA_HEAD_DIM), q, zero) for e in range(2)]
    scores = [[lax.dot_general(qe, kv[2 * i][...], (((1,), (1,)), ((), ())), preferred_element_type=F32)
               for i in range(nseg)] for qe in qs]
    vs = [jnp.concatenate([kv[2 * i + 1][...], jnp.ones(kv[2 * i + 1].shape, BF16)], axis=1) for i in range(nseg)]
    halves = []
    for e in range(2):
        m = scores[e][0].max(axis=-1, keepdims=True)
        for s in scores[e][1:]:
            m = jnp.maximum(m, s.max(axis=-1, keepdims=True))
        acc = jnp.zeros((tq, 2 * DA_V), F32)
        for i in range(nseg):
            p = jnp.exp2(scores[e][i] - m)
            acc = acc + jnp.dot(p.astype(BF16), vs[i], preferred_element_type=F32)
        halves.append(acc[:, :DA_V] / acc[:, DA_V:DA_V + 1])
    lp = lam_ref[...]
    lam = (jnp.exp(jnp.sum(lp[0:1] * lp[1:2], axis=-1, keepdims=True))
           - jnp.exp(jnp.sum(lp[2:3] * lp[3:4], axis=-1, keepdims=True)) + lam_init)
    o = halves[0] - lam * halves[1]
    o_ref[...] = (_rms(o) * ng_ref[...] * (1.0 - lam_init)).astype(o_ref.dtype)


def _attention(q, segs, lam_p, ng, lam_init):
    nb, t, _ = q.shape
    tq = _tile(t, ATTN_Q_TILE)
    in_specs = [_const_spec((4, DA_HEAD_DIM)), _const_spec((1, DA_V)),
                pl.BlockSpec((None, tq, DA_V), lambda b, h, i: (b, i, h))]
    args = [lam_p, ng, q]
    for k, v in segs:
        tk = k.shape[1]
        in_specs += [pl.BlockSpec((None, tk, DA_V), lambda b, h, i: (b, 0, h))] * 2
        args += [k, v]
    return pl.pallas_call(
        functools.partial(_attn_kernel, nseg=len(segs), lam_init=lam_init),
        grid=(nb, DA_HEADS, t // tq),
        in_specs=in_specs,
        out_specs=pl.BlockSpec((None, tq, DA_V), lambda b, h, i: (b, i, h)),
        out_shape=jax.ShapeDtypeStruct((nb, t, D), BF16),
        compiler_params=_params(("parallel", "parallel", "arbitrary")),
    )(*args)


def _attn_layer(x, xc, mod, layer, nb, g1, p, need_ctx):
    wq, wk, wv, lam_p, ng, wout = p
    lam_init = 0.8 - 0.6 * math.exp(-0.3 * layer)
    q_l, k_l, v_l = _qkv(x, mod, layer, None, g1, wq, wk, wv, _rope_tables(x.shape[1]))
    q_c, k_c, v_c = _qkv(xc, mod, layer, nb, g1, wq, wk, wv, None)
    zero_b = jnp.zeros((1, D), F32)
    o_l = _attention(q_l, [(k_c, v_c), (k_l, v_l)], lam_p, ng, lam_init)
    x = _resid_matmul(o_l, wout, zero_b, x, mod, layer, None, 2)
    if need_ctx:
        o_c = _attention(q_c, [(k_c, v_c)], lam_p, ng, lam_init)
        xc = _resid_matmul(o_c, wout, zero_b, xc, mod, layer, nb, 2)
    return x, xc


def _conf_kernel(xm_ref, xp_ref, xn_ref, mod_ref, g_ref, w1_ref, b1_ref, dw_ref, db_ref, lg_ref, lb_ref,
                 w2_ref, b2_ref, o_ref, u_ref, c_ref, sh_ref, *, tm, nt):
    t = pl.program_id(1)
    hl = CONF_HALO
    gs = g_ref[...] * (1.0 + mod_ref[1:2, :])
    shift = mod_ref[0:1, :]

    def glu(x, valid):
        h = (_rms(x) * gs + shift).astype(BF16)
        a = jnp.dot(h, w1_ref[...], preferred_element_type=F32) + b1_ref[...]
        u = a[:, :D] * _sigmoid(a[:, D:])
        return u if valid is None else jnp.where(valid, u, 0.0)

    u_ref[0:hl, :] = glu(xp_ref[...], t > 0)
    u_ref[hl:hl + tm, :] = glu(xm_ref[...], None)
    u_ref[hl + tm:, :] = glu(xn_ref[...], t < nt - 1)

    rb, cw = 64, 256
    pad = CONF_KERNEL // 2
    for c0 in range(0, D, cw):
        blk = u_ref[:, c0:c0 + cw]
        for r in range(1, 8):
            sh_ref[r - 1, :, c0:c0 + cw] = pltpu.roll(blk, r, 0)
    for r0 in range(0, tm, rb):
        for c0 in range(0, D, cw):
            acc = jnp.broadcast_to(db_ref[:, c0:c0 + cw], (rb, cw))
            for kk in range(CONF_KERNEL):
                d = kk - pad
                r = (-d) % 8
                s = hl + d + r + r0
                src = u_ref if r == 0 else sh_ref.at[r - 1]
                acc = acc + src[s:s + rb, c0:c0 + cw] * dw_ref[kk:kk + 1, c0:c0 + cw]
            c_ref[r0:r0 + rb, c0:c0 + cw] = acc

    cv = c_ref[...]
    mu = jnp.mean(cv, axis=-1, keepdims=True)
    xc = cv - mu
    ln = xc * lax.rsqrt(jnp.mean(xc * xc, axis=-1, keepdims=True) + EPS) * lg_ref[...] + lb_ref[...]
    y = jnp.dot(_silu(ln).astype(BF16), w2_ref[...], preferred_element_type=F32) + b2_ref[...]
    o_ref[...] = xm_ref[...] + mod_ref[2:3, :] * y


def _conformer(x, mod, layer, row, g, p):
    w1, b1, dw, db, lg, lb, w2, b2 = p
    nb, t, _ = x.shape
    tm = _tile(t, 256)
    nt = t // tm
    hl = CONF_HALO
    r = tm // hl
    return pl.pallas_call(
        functools.partial(_conf_kernel, tm=tm, nt=nt),
        grid=(nb, nt),
        in_specs=[pl.BlockSpec((None, tm, D), lambda b, i: (b, i, 0)),
                  pl.BlockSpec((None, hl, D), lambda b, i: (b, jnp.maximum(i * r - 1, 0), 0)),
                  pl.BlockSpec((None, hl, D), lambda b, i: (b, jnp.minimum((i + 1) * r, t // hl - 1), 0)),
                  _mod_spec(layer, row), _const_spec((1, D)),
                  _const_spec((D, 2 * D)), _const_spec((1, 2 * D)),
                  _const_spec((CONF_KERNEL, D)), _const_spec((1, D)), _const_spec((1, D)), _const_spec((1, D)),
                  _const_spec((D, D)), _const_spec((1, D))],
        out_specs=pl.BlockSpec((None, tm, D), lambda b, i: (b, i, 0)),
        out_shape=jax.ShapeDtypeStruct((nb, t, D), F32),
        scratch_shapes=[pltpu.VMEM((tm + 2 * hl, D), F32), pltpu.VMEM((tm, D), F32),
                        pltpu.VMEM((7, tm + 2 * hl, D), F32)],
        compiler_params=_params(("parallel", "parallel")),
    )(x, x, x, mod, g, w1, b1, dw, db, lg, lb, w2, b2)


def _ffn_kernel(*refs, tm, nt, final):
    if final:
        (xm_ref, xp_ref, xn_ref, mod_ref, g_ref, wu_ref, cw_ref, cb_ref, wd_ref, fg_ref,
         o_ref, h_ref, u_ref, act_ref) = refs
    else:
        (xm_ref, xp_ref, xn_ref, mod_ref, g_ref, wu_ref, cw_ref, cb_ref, wd_ref,
         o_ref, h_ref, u_ref, act_ref) = refs
    t = pl.program_id(1)
    hl = FFN_HALO
    gs = g_ref[...] * (1.0 + mod_ref[4:5, :])
    shift = mod_ref[3:4, :]
    mod = lambda x: _rms(x) * gs + shift
    hp = jnp.where(t > 0, mod(xp_ref[...]), 0.0)
    hn = jnp.where(t < nt - 1, mod(xn_ref[...]), 0.0)
    h_ref[...] = jnp.concatenate([hp, mod(xm_ref[...]), hn], axis=0).astype(BF16)

    for j in range(FFN_NCHUNK):
        ub = u_ref.at[j % 2]
        ub[...] = jnp.dot(h_ref[...], wu_ref[j], preferred_element_type=F32)
        w = cw_ref[j]
        cv = (cb_ref[j] + w[0:1, :] * ub[hl - 1:hl - 1 + tm, :] + w[1:2, :] * ub[hl:hl + tm, :]
              + w[2:3, :] * ub[hl + 1:hl + 1 + tm, :])
        act_ref[:, j * FFN_CHUNK:(j + 1) * FFN_CHUNK] = (_silu(cv[:, :FFN_CHUNK]) * cv[:, FFN_CHUNK:]).astype(BF16)

    y = jnp.dot(act_ref[...], wd_ref[...], preferred_element_type=F32)
    out = xm_ref[...] + mod_ref[5:6, :] * y
    if final:
        out = _rms(out) * fg_ref[...]
    o_ref[...] = out


def _ffn(x, mod, layer, row, g, p, final_g=None):
    wu, cw, cb, wd = p
    nb, t, _ = x.shape
    tm = _tile(t, TOKEN_TILE)
    nt = t // tm
    hl = FFN_HALO
    r = tm // hl
    final = final_g is not None
    in_specs = [pl.BlockSpec((None, tm, D), lambda b, i: (b, i, 0)),
                pl.BlockSpec((None, hl, D), lambda b, i: (b, jnp.maximum(i * r - 1, 0), 0)),
                pl.BlockSpec((None, hl, D), lambda b, i: (b, jnp.minimum((i + 1) * r, t // hl - 1), 0)),
                _mod_spec(layer, row), _const_spec((1, D)),
                _const_spec((FFN_NCHUNK, D, 2 * FFN_CHUNK)), _const_spec((FFN_NCHUNK, 3, 2 * FFN_CHUNK)),
                _const_spec((FFN_NCHUNK, 1, 2 * FFN_CHUNK)), _const_spec((FFN_HIDDEN, D))]
    args = [x, x, x, mod, g, wu, cw, cb, wd]
    if final:
        in_specs.append(_const_spec((1, D)))
        args.append(final_g)
    return pl.pallas_call(
        functools.partial(_ffn_kernel, tm=tm, nt=nt, final=final),
        grid=(nb, nt),
        in_specs=in_specs,
        out_specs=pl.BlockSpec((None, tm, D), lambda b, i: (b, i, 0)),
        out_shape=jax.ShapeDtypeStruct((nb, t, D), F32),
        scratch_shapes=[pltpu.VMEM((tm + 2 * hl, D), BF16), pltpu.VMEM((2, tm + 2 * hl, 2 * FFN_CHUNK), F32),
                        pltpu.VMEM((tm, FFN_HIDDEN), BF16)],
        compiler_params=_params(("parallel", "parallel")),
    )(*args)


def _ffn_weights(w_up, conv_w, conv_b, w_down):
    def pair(a):
        lead = a.shape[:-1]
        a = a.reshape(lead + (2, FFN_NCHUNK, FFN_CHUNK))
        a = jnp.moveaxis(a, -2, 0)
        return a.reshape((FFN_NCHUNK,) + lead + (2 * FFN_CHUNK,))
    return (pair(w_up).astype(BF16), pair(conv_w), pair(conv_b[None, :]),
            w_down.astype(BF16))


def kernel(x, c, ctx, c_ctx, mod_w, mod_b, norm1_g, norm2_g, ffn_w_up, ffn_conv_w, ffn_conv_b, ffn_w_down,
           ssm_w_in, ssm_conv_w, ssm_conv_b, ssm_dt_bias, ssm_a_log, ssm_d, ssm_norm_g, ssm_w_out, attn_w_in,
           attn_lambda, attn_norm_g, attn_w_out, conf_w_pw1, conf_b_pw1, conf_dw_w, conf_dw_b, conf_ln_g,
           conf_ln_b, conf_w_pw2, conf_b_pw2, final_g):
    depth = mod_w.shape[0]
    nb = x.shape[0]
    mod = _mod_table(c, c_ctx, mod_w, mod_b)
    xc = ctx
    row = lambda a: a.reshape(1, -1)
    pad_dt = lambda a: jnp.pad(a.reshape(1, -1), ((0, 0), (0, DT_PAD - 2 * SSM_HEADS)))
    for i in range(depth):
        kind, j = i % N_MIXERS, i // N_MIXERS
        need_ctx = i < depth - 1
        g1 = row(norm1_g[i])
        if kind == 0:
            w_in = ssm_w_in[j]
            o_x, o_dt = D_INNER, D_INNER + SSM_CONV_CH
            wd = jnp.pad(w_in[:, o_dt:], ((0, 0), (0, DT_PAD - 2 * SSM_HEADS)))
            p = (w_in[:, :o_x].astype(BF16), w_in[:, o_x:o_dt].astype(BF16), wd.astype(BF16),
                 ssm_conv_w[j], ssm_conv_b[j][:, None, :], pad_dt(ssm_dt_bias[j]), pad_dt(ssm_a_log[j]),
                 jnp.repeat(ssm_d[j], SSM_HEAD_DIM, axis=-1)[:, None, :], row(ssm_norm_g[j]),
                 ssm_w_out[j].astype(BF16))
            x, xc = _mamba_layer(x, xc, mod, i, nb, g1, p, need_ctx)
        elif kind == 1:
            w_in = attn_w_in[j].astype(BF16)
            p = (w_in[:, :D], w_in[:, D:2 * D], w_in[:, 2 * D:], attn_lambda[j], row(attn_norm_g[j]),
                 attn_w_out[j].astype(BF16))
            x, xc = _attn_layer(x, xc, mod, i, nb, g1, p, need_ctx)
        else:
            p = (conf_w_pw1[j].astype(BF16), row(conf_b_pw1[j]), conf_dw_w[j], row(conf_dw_b[j]),
                 row(conf_ln_g[j]), row(conf_ln_b[j]), conf_w_pw2[j].astype(BF16), row(conf_b_pw2[j]))
            x = _conformer(x, mod, i, None, g1, p)
            if need_ctx:
                xc = _conformer(xc, mod, i, nb, g1, p)
        fp = _ffn_weights(ffn_w_up[i], ffn_conv_w[i], ffn_conv_b[i], ffn_w_down[i])
        g2 = row(norm2_g[i])
        x = _ffn(x, mod, i, None, g2, fp, final_g=row(final_g) if i == depth - 1 else None)
        if need_ctx:
            xc = _ffn(xc, mod, i, nb, g2, fp)
    return x
```

```python
import functools
import math

import jax
import jax.numpy as jnp
from jax import lax
from jax.experimental import pallas as pl
from jax.experimental.pallas import tpu as pltpu

F32 = jnp.float32
BF16 = jnp.bfloat16
HIGHEST = lax.Precision.HIGHEST

D = 1024
N_MOD = 6
N_MIXERS = 3
EPS = 1e-6
GRID_W = 64
ROPE_THETA = 10000.0

SSM_HEAD_DIM = 64
SSM_HEADS = 32
SSM_GROUPS = 8
SSM_HPG = SSM_HEADS // SSM_GROUPS
SSM_STATE = 128
SSM_CONV = 4
SSM_CHUNK = 128
D_INNER = SSM_HEADS * SSM_HEAD_DIM
SSM_GW = SSM_HPG * SSM_HEAD_DIM
SSM_BC = SSM_GROUPS * SSM_STATE
SSM_CONV_CH = D_INNER + 2 * SSM_BC
DT_PAD = 128
MAMBA_HALO = 8
MAMBA_COLS = 512
MAMBA_ZCOLS = D_INNER * MAMBA_COLS // SSM_CONV_CH
SSD_HALO = 16

DA_HEAD_DIM = 64
DA_HEADS = D // (2 * DA_HEAD_DIM)
DA_V = 2 * DA_HEAD_DIM

CONF_KERNEL = 31
CONF_HALO = 16

FFN_HIDDEN = 2816
FFN_CHUNK = 256
FFN_NCHUNK = FFN_HIDDEN // FFN_CHUNK
FFN_HALO = 8
FFN_DOWN_EVERY = 4

VMEM_LIMIT = 56 * 1024 * 1024
TOKEN_TILE = 512
ATTN_Q_TILE = 512


def _sigmoid(x):
    return 1.0 / (1.0 + jnp.exp(-x))


def _silu(x):
    return x * _sigmoid(x)


def _softplus(x):
    return jnp.maximum(x, 0.0) + jnp.log(1.0 + jnp.exp(-jnp.abs(x)))


def _rms(x):
    return x * lax.rsqrt(jnp.mean(x * x, axis=-1, keepdims=True) + EPS)


def _shift_tiles(x3, s, first, count):
    if s == 0:
        return x3[first:first + count]
    sub = lax.broadcasted_iota(jnp.int32, (count,) + x3.shape[1:], 1)
    rk = pltpu.roll(x3, (-s) % 8, 1)
    if s < 0:
        return jnp.where(sub >= -s, rk[first:first + count], rk[first - 1:first - 1 + count])
    return jnp.where(sub < 8 - s, rk[first:first + count], rk[first + 1:first + 1 + count])


def _params(sem, vmem=VMEM_LIMIT):
    return pltpu.CompilerParams(dimension_semantics=sem, vmem_limit_bytes=vmem)


def _mod_spec(layer, row):
    if row is None:
        return pl.BlockSpec((None, None, N_MOD, D), lambda b, *_: (layer, b, 0, 0))
    return pl.BlockSpec((None, None, N_MOD, D), lambda b, *_: (layer, row, 0, 0))


def _const_spec(shape):
    nd = len(shape)
    return pl.BlockSpec(shape, lambda *_: (0,) * nd)


def _tile(n, pref):
    return pref if n % pref == 0 else n


def _mod_kernel(s_ref, w_ref, b_ref, o_ref):
    s = _silu(s_ref[...])
    o_ref[...] = jnp.dot(s, w_ref[...], preferred_element_type=F32, precision=HIGHEST) + b_ref[...]


def _mod_table(c, c_ctx, mod_w, mod_b):
    depth = mod_w.shape[0]
    nb = c.shape[0]
    rows = -(-(nb + 1) // 8) * 8
    s = jnp.zeros((rows, D), F32).at[:nb].set(c).at[nb].set(c_ctx)
    tn = 1536
    out = pl.pallas_call(
        _mod_kernel,
        grid=(depth, N_MOD * D // tn),
        in_specs=[pl.BlockSpec((rows, D), lambda i, n: (0, 0)),
                  pl.BlockSpec((None, D, tn), lambda i, n: (i, 0, n)),
                  pl.BlockSpec((None, 1, tn), lambda i, n: (i, 0, n))],
        out_specs=pl.BlockSpec((None, rows, tn), lambda i, n: (i, 0, n)),
        out_shape=jax.ShapeDtypeStruct((depth, rows, N_MOD * D), F32),
        compiler_params=_params(("arbitrary", "arbitrary")),
    )(s, mod_w, mod_b.reshape(depth, 1, N_MOD * D))
    return out.reshape(depth, rows, N_MOD, D)


def _mamba_in_kernel(xm_ref, xp_ref, xn_ref, mod_ref, g_ref, wxz_ref, wd_ref, cw_ref, cb_ref,
                     z_ref, uf_ref, xbc_ref, dt_ref, he_ref, *, tm, nt):
    t = pl.program_id(1)
    hl = MAMBA_HALO
    gs = g_ref[...] * (1.0 + mod_ref[1:2, :])
    shift = mod_ref[0:1, :]
    mod = lambda x: _rms(x) * gs + shift
    hm = mod(xm_ref[...])
    hp = jnp.where(t > 0, mod(xp_ref[...]), 0.0)
    hn = jnp.where(t < nt - 1, mod(xn_ref[...]), 0.0)
    he_ref[...] = jnp.concatenate([hp, hm, hn], axis=0).astype(BF16)
    dt_ref[...] = jnp.dot(he_ref[...], wd_ref[...], preferred_element_type=F32)[hl:hl + tm]
    nt8, nm8 = (tm + 2 * hl) // 8, tm // 8
    npass = SSM_CONV_CH // MAMBA_COLS
    proj = lambda j: jnp.dot(he_ref[...], wxz_ref[j], preferred_element_type=F32)
    r_next = proj(0)
    for j in range(npass):
        cols = slice(j * MAMBA_COLS, (j + 1) * MAMBA_COLS)
        r = r_next
        if j + 1 < npass:
            r_next = proj(j + 1)
        z_ref[:, j * MAMBA_ZCOLS:(j + 1) * MAMBA_ZCOLS] = r[hl:hl + tm, MAMBA_COLS:].astype(z_ref.dtype)
        xe = r[:, :MAMBA_COLS]
        xbc_ref[:, cols] = xe[hl:hl + tm].astype(xbc_ref.dtype)
        x3 = xe.reshape(nt8, 8, MAMBA_COLS)
        acc = cb_ref[:, cols]
        for kk in range(SSM_CONV):
            acc = acc + _shift_tiles(x3, kk - (SSM_CONV - 1), 1, nm8) * cw_ref[kk:kk + 1, cols]
        uf_ref[:, cols] = _silu(acc).reshape(tm, MAMBA_COLS).astype(uf_ref.dtype)


def _mamba_in(x, mod, layer, row, g, wxz, wd, cw, cb):
    nb, t, _ = x.shape
    npass = SSM_CONV_CH // MAMBA_COLS
    tm = _tile(t, TOKEN_TILE)
    nt = t // tm
    hl = MAMBA_HALO
    r = tm // hl
    tok = lambda w: pl.BlockSpec((None, tm, w), lambda b, i: (b, i, 0))
    return pl.pallas_call(
        functools.partial(_mamba_in_kernel, tm=tm, nt=nt),
        grid=(nb, nt),
        in_specs=[tok(D),
                  pl.BlockSpec((None, hl, D), lambda b, i: (b, jnp.maximum(i * r - 1, 0), 0)),
                  pl.BlockSpec((None, hl, D), lambda b, i: (b, jnp.minimum((i + 1) * r, t // hl - 1), 0)),
                  _mod_spec(layer, row), _const_spec((1, D)),
                  _const_spec((npass, D, MAMBA_COLS + MAMBA_ZCOLS)), _const_spec((D, DT_PAD)),
                  _const_spec((SSM_CONV, SSM_CONV_CH)), _const_spec((1, SSM_CONV_CH))],
        out_specs=[tok(D_INNER), tok(SSM_CONV_CH), tok(SSM_CONV_CH), tok(DT_PAD)],
        out_shape=[jax.ShapeDtypeStruct((nb, t, D_INNER), BF16),
                   jax.ShapeDtypeStruct((nb, t, SSM_CONV_CH), BF16),
                   jax.ShapeDtypeStruct((nb, t, SSM_CONV_CH), BF16),
                   jax.ShapeDtypeStruct((nb, t, DT_PAD), F32)],
        scratch_shapes=[pltpu.VMEM((tm + 2 * hl, D), BF16)],
        compiler_params=_params(("parallel", "parallel")),
    )(x, x, x, mod, g, wxz, wd, cw, cb)


def _ssd_kernel(*refs, rev, nc, write_y, combine, conv):
    it = iter(refs)
    u_ref = next(it)
    if conv:
        xh_ref, cw_ref, cbias_ref = (next(it) for _ in range(3))
    dt_ref, dtb_ref, alog_ref, dsk_ref, h0_ref = (next(it) for _ in range(5))
    if combine:
        yf_ref, z_ref, ng_ref = (next(it) for _ in range(3))
    y_ref = next(it) if write_y else None
    hfin_ref = next(it)
    ht_ref = next(it)

    k = pl.program_id(1)
    ch = SSM_CHUNK
    off = SSM_HEADS if rev else 0

    @pl.when(k == 0)
    def _():
        ht_ref[...] = h0_ref[...]

    if conv:
        c = (nc - 1 - k) if rev else k
        side_on = (c < nc - 1) if rev else (c > 0)

        def load(c0, w):
            main = u_ref[:, c0:c0 + w].astype(F32)
            side = jnp.where(side_on, xh_ref[:, c0:c0 + w].astype(F32), 0.0)
            buf = jnp.concatenate([main, side] if rev else [side, main], axis=0)
            x3 = buf.reshape((ch + SSD_HALO) // 8, 8, w)
            first = 0 if rev else SSD_HALO // 8
            acc = cbias_ref[:, c0:c0 + w]
            for kk in range(SSM_CONV):
                s = (SSM_CONV - 1 - kk) * (1 if rev else -1)
                acc = acc + _shift_tiles(x3, s, first, ch // 8) * cw_ref[kk:kk + 1, c0:c0 + w]
            return _silu(acc).reshape(ch, w)
    else:
        def load(c0, w):
            return u_ref[:, c0:c0 + w]

    dtv = _softplus(dt_ref[...] + dtb_ref[...])
    la = dtv * (-jnp.exp(alog_ref[...]))
    ri = lax.broadcasted_iota(jnp.int32, (ch, ch), 0)
    ci = lax.broadcasted_iota(jnp.int32, (ch, ch), 1)
    keep = (ci >= ri) if rev else (ci <= ri)
    acum = jnp.dot(keep.astype(F32), la, preferred_element_type=F32, precision=HIGHEST)
    last = 0 if rev else ch - 1
    total = acum[last:last + 1, :]
    wcol = jnp.exp(total - acum) * dtv
    offv = jnp.exp(acum)
    acum_t = acum.T
    dt_t = dtv.T

    lane = lax.broadcasted_iota(jnp.int32, (ch, SSM_GW), 1)

    def expand(colmat, g):
        out = jnp.broadcast_to(colmat[:, off + SSM_HPG * g:off + SSM_HPG * g + 1], (ch, SSM_GW))
        for j in range(1, SSM_HPG):
            col = off + SSM_HPG * g + j
            out = jnp.where(lane >= j * SSM_HEAD_DIM, jnp.broadcast_to(colmat[:, col:col + 1], (ch, SSM_GW)), out)
        return out

    for g in range(SSM_GROUPS):
        xg = load(g * SSM_GW, SSM_GW)
        xg16, xg = xg.astype(BF16), xg.astype(F32)
        bg16 = load(D_INNER + g * SSM_STATE, SSM_STATE).astype(BF16)
        xw = (xg * expand(wcol, g)).astype(BF16)
        st_t = lax.dot_general(bg16, xw, (((0,), (0,)), ((), ())), preferred_element_type=F32)
        sg = expand(offv, g)
        ht_old = ht_ref[g]
        if write_y:
            cg16 = load(D_INNER + SSM_BC + g * SSM_STATE, SSM_STATE).astype(BF16)
            cbm = lax.dot_general(cg16, bg16, (((1,), (1,)), ((), ())), preferred_element_type=F32)
            yoff = jnp.dot(cg16, ht_old.astype(BF16), preferred_element_type=F32)
            ms, xs = [], []
            for j in range(SSM_HPG):
                col = off + SSM_HPG * g + j
                seg = acum[:, col:col + 1] - acum_t[col:col + 1, :]
                lmat = jnp.exp(jnp.where(keep, seg, -jnp.inf))
                ms.append((cbm * lmat * dt_t[col:col + 1, :]).astype(BF16))
                in_head = (lane >= j * SSM_HEAD_DIM) & (lane < (j + 1) * SSM_HEAD_DIM)
                xs.append(jnp.where(in_head, xg16, jnp.zeros_like(xg16)))
            ydiag = jnp.dot(jnp.concatenate(ms, axis=1), jnp.concatenate(xs, axis=0),
                            preferred_element_type=F32)
            y = ydiag + yoff * sg + dsk_ref[:, g * SSM_GW:(g + 1) * SSM_GW] * xg
            cols = slice(g * SSM_GW, (g + 1) * SSM_GW)
            if combine:
                y = (y + yf_ref[:, cols].astype(F32)) * _silu(z_ref[:, cols].astype(F32))
                y = _rms(y) * ng_ref[:, cols]
            y_ref[:, cols] = y.astype(y_ref.dtype)
        ht_ref[g] = ht_old * sg[last:last + 1, :] + st_t

    @pl.when(k == nc - 1)
    def _():
        hfin_ref[...] = ht_ref[...]


def _ssd(u, dt, h0, dt_bias, a_log, d_skip, *, rev, write_y, taps=None, combine=None, y_dtype=F32):
    nb, t, _ = u.shape
    nc = t // SSM_CHUNK
    hb = SSM_CHUNK // SSD_HALO
    if rev:
        cidx = lambda k: nc - 1 - k
        hidx = lambda k: jnp.minimum((nc - k) * hb, t // SSD_HALO - 1)
    else:
        cidx = lambda k: k
        hidx = lambda k: jnp.maximum(k * hb - 1, 0)
    chunk = lambda w: pl.BlockSpec((None, SSM_CHUNK, w), lambda b, k: (b, cidx(k), 0))
    state = pl.BlockSpec((None, SSM_GROUPS, SSM_STATE, SSM_GW), lambda b, k: (b, 0, 0, 0))
    in_specs, args = [chunk(SSM_CONV_CH)], [u]
    scratch = [pltpu.VMEM((SSM_GROUPS, SSM_STATE, SSM_GW), F32)]
    if taps is not None:
        in_specs += [pl.BlockSpec((None, SSD_HALO, SSM_CONV_CH), lambda b, k: (b, hidx(k), 0)),
                     _const_spec((SSM_CONV, SSM_CONV_CH)), _const_spec((1, SSM_CONV_CH))]
        args += [u, taps[0], taps[1]]
    in_specs += [chunk(DT_PAD), _const_spec((1, DT_PAD)), _const_spec((1, DT_PAD)), _const_spec((1, D_INNER)), state]
    args += [dt, dt_bias, a_log, d_skip, h0]
    out_specs, out_shape = [], []
    if combine is not None:
        yf, z, ng = combine
        in_specs += [chunk(D_INNER), chunk(D_INNER), _const_spec((1, D_INNER))]
        args += [yf, z, ng]
    if write_y:
        out_specs.append(chunk(D_INNER))
        out_shape.append(jax.ShapeDtypeStruct((nb, t, D_INNER), y_dtype))
    out_specs.append(state)
    out_shape.append(jax.ShapeDtypeStruct((nb, SSM_GROUPS, SSM_STATE, SSM_GW), F32))
    outs = pl.pallas_call(
        functools.partial(_ssd_kernel, rev=rev, nc=nc, write_y=write_y, combine=combine is not None,
                          conv=taps is not None),
        grid=(nb, nc),
        in_specs=in_specs, out_specs=out_specs, out_shape=out_shape,
        scratch_shapes=scratch,
        compiler_params=_params(("parallel", "arbitrary")),
    )(*args)
    return outs if write_y else (None, outs[0])


def _resid_kernel(a_ref, w_ref, b_ref, x_ref, mod_ref, o_ref, *, gate_row):
    y = jnp.dot(a_ref[...], w_ref[...], preferred_element_type=F32) + b_ref[...]
    o_ref[...] = x_ref[...] + mod_ref[gate_row:gate_row + 1, :] * y


def _resid_matmul(a, w, bias, x, mod, layer, row, gate_row):
    nb, t, kdim = a.shape
    tm = _tile(t, TOKEN_TILE)
    tok = lambda wd: pl.BlockSpec((None, tm, wd), lambda b, i: (b, i, 0))
    return pl.pallas_call(
        functools.partial(_resid_kernel, gate_row=gate_row),
        grid=(nb, t // tm),
        in_specs=[tok(kdim), _const_spec((kdim, D)), _const_spec((1, D)), tok(D), _mod_spec(layer, row)],
        out_specs=tok(D),
        out_shape=jax.ShapeDtypeStruct((nb, t, D), F32),
        compiler_params=_params(("parallel", "parallel")),
    )(a, w, bias, x, mod)


def _mamba_layer(x, xc, mod, layer, nb, g1, p, need_ctx):
    wxz, wd, cw, cbias, dtb, alog, dsk, ng, wout = p
    z_l, uf_l, xbc_l, dt_l = _mamba_in(x, mod, layer, None, g1, wxz, wd, cw[0], cbias[0])
    z_c, uf_c, xbc_c, dt_c = _mamba_in(xc, mod, layer, nb, g1, wxz, wd, cw[0], cbias[0])
    h0 = jnp.zeros((x.shape[0], SSM_GROUPS, SSM_STATE, SSM_GW), F32)
    dirp = lambda d: (dtb, alog, dsk[d])
    yf_c, hf = _ssd(uf_c, dt_c, h0, *dirp(0), rev=False, write_y=need_ctx)
    yf_l, _ = _ssd(uf_l, dt_l, hf, *dirp(0), rev=False, write_y=True)
    taps = (cw[1], cbias[1])
    yn_c, hb = _ssd(xbc_c, dt_c, h0, *dirp(1), rev=True, write_y=need_ctx, taps=taps,
                    combine=(yf_c, z_c, ng) if need_ctx else None, y_dtype=BF16)
    yn_l, _ = _ssd(xbc_l, dt_l, hb, *dirp(1), rev=True, write_y=True, taps=taps, combine=(yf_l, z_l, ng),
                   y_dtype=BF16)
    zero_b = jnp.zeros((1, D), F32)
    x = _resid_matmul(yn_l, wout, zero_b, x, mod, layer, None, 2)
    if need_ctx:
        xc = _resid_matmul(yn_c, wout, zero_b, xc, mod, layer, nb, 2)
    return x, xc


def _qkv_kernel(*refs, rope):
    if rope:
        x_ref, mod_ref, g_ref, wq_ref, wk_ref, wv_ref, cos_ref, sin_ref, q_ref, k_ref, v_ref = refs
    else:
        x_ref, mod_ref, g_ref, wq_ref, wk_ref, wv_ref, q_ref, k_ref, v_ref = refs
    gs = g_ref[...] * (1.0 + mod_ref[1:2, :])
    h = (_rms(x_ref[...]) * gs + mod_ref[0:1, :]).astype(BF16)
    tm = h.shape[0]
    if rope:
        cos, sin = cos_ref[...], sin_ref[...]
        lane = lax.broadcasted_iota(jnp.int32, (tm, 128), 1)
        first = (lane % 32) < 16

    def rot(a_ref, w_ref, scale):
        for j in range(D // 128):
            blk = jnp.dot(h, w_ref[:, j * 128:(j + 1) * 128], preferred_element_type=F32)
            if scale != 1.0:
                blk = blk * scale
            if rope:
                partner = jnp.where(first, pltpu.roll(blk, 112, 1), pltpu.roll(blk, 16, 1))
                blk = blk * cos + partner * sin
            a_ref[:, j * 128:(j + 1) * 128] = blk.astype(a_ref.dtype)

    rot(q_ref, wq_ref, DA_HEAD_DIM ** -0.5 * math.log2(math.e))
    rot(k_ref, wk_ref, 1.0)
    v_ref[...] = jnp.dot(h, wv_ref[...], preferred_element_type=F32).astype(v_ref.dtype)


def _rope_tables(s):
    lane = jnp.arange(128)
    pos = jnp.arange(s)
    coord = jnp.where(((lane % 64) // 32 == 0)[None, :], (pos // GRID_W)[:, None], (pos % GRID_W)[:, None])
    n_freq = DA_HEAD_DIM // 4
    inv = ROPE_THETA ** (-(lane % n_freq).astype(F32) / n_freq)
    ang = coord.astype(F32) * inv[None, :]
    sign = jnp.where((lane % 32) < 16, -1.0, 1.0)
    return jnp.cos(ang), jnp.sin(ang) * sign[None, :]


def _qkv(x, mod, layer, row, g, wq, wk, wv, tables):
    nb, t, _ = x.shape
    tm = _tile(t, TOKEN_TILE)
    tok = lambda: pl.BlockSpec((None, tm, D), lambda b, i: (b, i, 0))
    in_specs = [tok(), _mod_spec(layer, row), _const_spec((1, D))] + [_const_spec((D, D))] * 3
    args = [x, mod, g, wq, wk, wv]
    if tables is not None:
        in_specs += [pl.BlockSpec((tm, 128), lambda b, i: (i, 0))] * 2
        args += list(tables)
    return pl.pallas_call(
        functools.partial(_qkv_kernel, rope=tables is not None),
        grid=(nb, t // tm),
        in_specs=in_specs, out_specs=[tok()] * 3,
        out_shape=[jax.ShapeDtypeStruct((nb, t, D), BF16)] * 3,
        compiler_params=_params(("parallel", "parallel")),
    )(*args)


def _attn_kernel(*refs, nseg, lam_init):
    lam_ref, ng_ref, q_ref = refs[:3]
    kv = refs[3:3 + 2 * nseg]
    o_ref = refs[3 + 2 * nseg]
    q = q_ref[...]
    tq = q.shape[0]
    lane = lax.broadcasted_iota(jnp.int32, q.shape, 1)
    zero = jnp.zeros_like(q)
    qs = [jnp.where((lane >= e * DA_HEAD_DIM) & (lane < (e + 1) * DA_HEAD_DIM), q, zero) for e in range(2)]
    scores = [[lax.dot_general(qe, kv[2 * i][...], (((1,), (1,)), ((), ())), preferred_element_type=F32)
               for i in range(nseg)] for qe in qs]
    vs = [jnp.concatenate([kv[2 * i + 1][...], jnp.ones(kv[2 * i + 1].shape, BF16)], axis=1) for i in range(nseg)]
    halves = []
    for e in range(2):
        m = scores[e][0].max(axis=-1, keepdims=True)
        for s in scores[e][1:]:
            m = jnp.maximum(m, s.max(axis=-1, keepdims=True))
        acc = jnp.zeros((tq, 2 * DA_V), F32)
        for i in range(nseg):
            p = jnp.exp2(scores[e][i] - m)
            acc = acc + jnp.dot(p.astype(BF16), vs[i], preferred_element_type=F32)
        halves.append(acc[:, :DA_V] / acc[:, DA_V:DA_V + 1])
    lp = lam_ref[...]
    lam = (jnp.exp(jnp.sum(lp[0:1] * lp[1:2], axis=-1, keepdims=True))
           - jnp.exp(jnp.sum(lp[2:3] * lp[3:4], axis=-1, keepdims=True)) + lam_init)
    o = halves[0] - lam * halves[1]
    o_ref[...] = (_rms(o) * ng_ref[...] * (1.0 - lam_init)).astype(o_ref.dtype)


def _attention(q, segs, lam_p, ng, lam_init):
    nb, t, _ = q.shape
    tq = _tile(t, ATTN_Q_TILE)
    in_specs = [_const_spec((4, DA_HEAD_DIM)), _const_spec((1, DA_V)),
                pl.BlockSpec((None, tq, DA_V), lambda b, h, i: (b, i, h))]
    args = [lam_p, ng, q]
    for k, v in segs:
        tk = k.shape[1]
        in_specs += [pl.BlockSpec((None, tk, DA_V), lambda b, h, i: (b, 0, h))] * 2
        args += [k, v]
    return pl.pallas_call(
        functools.partial(_attn_kernel, nseg=len(segs), lam_init=lam_init),
        grid=(nb, DA_HEADS, t // tq),
        in_specs=in_specs,
        out_specs=pl.BlockSpec((None, tq, DA_V), lambda b, h, i: (b, i, h)),
        out_shape=jax.ShapeDtypeStruct((nb, t, D), BF16),
        compiler_params=_params(("parallel", "parallel", "arbitrary")),
    )(*args)


def _attn_layer(x, xc, mod, layer, nb, g1, p, need_ctx):
    wq, wk, wv, lam_p, ng, wout = p
    lam_init = 0.8 - 0.6 * math.exp(-0.3 * layer)
    q_l, k_l, v_l = _qkv(x, mod, layer, None, g1, wq, wk, wv, _rope_tables(x.shape[1]))
    q_c, k_c, v_c = _qkv(xc, mod, layer, nb, g1, wq, wk, wv, None)
    zero_b = jnp.zeros((1, D), F32)
    o_l = _attention(q_l, [(k_c, v_c), (k_l, v_l)], lam_p, ng, lam_init)
    x = _resid_matmul(o_l, wout, zero_b, x, mod, layer, None, 2)
    if need_ctx:
        o_c = _attention(q_c, [(k_c, v_c)], lam_p, ng, lam_init)
        xc = _resid_matmul(o_c, wout, zero_b, xc, mod, layer, nb, 2)
    return x, xc


def _conf_kernel(xm_ref, xp_ref, xn_ref, mod_ref, g_ref, w1_ref, b1_ref, dw_ref, db_ref, lg_ref, lb_ref,
                 w2_ref, b2_ref, o_ref, u_ref, c_ref, sh_ref, *, tm, nt):
    t = pl.program_id(1)
    hl = CONF_HALO
    gs = g_ref[...] * (1.0 + mod_ref[1:2, :])
    shift = mod_ref[0:1, :]

    def glu(x, valid):
        h = (_rms(x) * gs + shift).astype(BF16)
        a = jnp.dot(h, w1_ref[...], preferred_element_type=F32) + b1_ref[...]
        u = a[:, :D] * _sigmoid(a[:, D:])
        return u if valid is None else jnp.where(valid, u, 0.0)

    u_ref[0:hl, :] = glu(xp_ref[...], t > 0)
    u_ref[hl:hl + tm, :] = glu(xm_ref[...], None)
    u_ref[hl + tm:, :] = glu(xn_ref[...], t < nt - 1)

    rb, cw = 64, 256
    pad = CONF_KERNEL // 2
    for c0 in range(0, D, cw):
        blk = u_ref[:, c0:c0 + cw]
        for r in range(1, 8):
            sh_ref[r - 1, :, c0:c0 + cw] = pltpu.roll(blk, r, 0)
    for r0 in range(0, tm, rb):
        for c0 in range(0, D, cw):
            acc = jnp.broadcast_to(db_ref[:, c0:c0 + cw], (rb, cw))
            for kk in range(CONF_KERNEL):
                d = kk - pad
                r = (-d) % 8
                s = hl + d + r + r0
                src = u_ref if r == 0 else sh_ref.at[r - 1]
                acc = acc + src[s:s + rb, c0:c0 + cw] * dw_ref[kk:kk + 1, c0:c0 + cw]
            c_ref[r0:r0 + rb, c0:c0 + cw] = acc

    cv = c_ref[...]
    mu = jnp.mean(cv, axis=-1, keepdims=True)
    xc = cv - mu
    ln = xc * lax.rsqrt(jnp.mean(xc * xc, axis=-1, keepdims=True) + EPS) * lg_ref[...] + lb_ref[...]
    y = jnp.dot(_silu(ln).astype(BF16), w2_ref[...], preferred_element_type=F32) + b2_ref[...]
    o_ref[...] = xm_ref[...] + mod_ref[2:3, :] * y


def _conformer(x, mod, layer, row, g, p):
    w1, b1, dw, db, lg, lb, w2, b2 = p
    nb, t, _ = x.shape
    tm = _tile(t, 256)
    nt = t // tm
    hl = CONF_HALO
    r = tm // hl
    return pl.pallas_call(
        functools.partial(_conf_kernel, tm=tm, nt=nt),
        grid=(nb, nt),
        in_specs=[pl.BlockSpec((None, tm, D), lambda b, i: (b, i, 0)),
                  pl.BlockSpec((None, hl, D), lambda b, i: (b, jnp.maximum(i * r - 1, 0), 0)),
                  pl.BlockSpec((None, hl, D), lambda b, i: (b, jnp.minimum((i + 1) * r, t // hl - 1), 0)),
                  _mod_spec(layer, row), _const_spec((1, D)),
                  _const_spec((D, 2 * D)), _const_spec((1, 2 * D)),
                  _const_spec((CONF_KERNEL, D)), _const_spec((1, D)), _const_spec((1, D)), _const_spec((1, D)),
                  _const_spec((D, D)), _const_spec((1, D))],
        out_specs=pl.BlockSpec((None, tm, D), lambda b, i: (b, i, 0)),
        out_shape=jax.ShapeDtypeStruct((nb, t, D), F32),
        scratch_shapes=[pltpu.VMEM((tm + 2 * hl, D), F32), pltpu.VMEM((tm, D), F32),
                        pltpu.VMEM((7, tm + 2 * hl, D), F32)],
        compiler_params=_params(("parallel", "parallel")),
    )(x, x, x, mod, g, w1, b1, dw, db, lg, lb, w2, b2)


def _ffn_kernel(*refs, tm, nt, final):
    if final:
        (xm_ref, xp_ref, xn_ref, mod_ref, g_ref, wu_ref, cw_ref, cb_ref, wd_ref, fg_ref,
         o_ref, h_ref, act_ref) = refs
    else:
        (xm_ref, xp_ref, xn_ref, mod_ref, g_ref, wu_ref, cw_ref, cb_ref, wd_ref,
         o_ref, h_ref, act_ref) = refs
    t = pl.program_id(1)
    hl = FFN_HALO
    gs = g_ref[...] * (1.0 + mod_ref[4:5, :])
    shift = mod_ref[3:4, :]
    mod = lambda x: _rms(x) * gs + shift
    hp = jnp.where(t > 0, mod(xp_ref[...]), 0.0)
    hn = jnp.where(t < nt - 1, mod(xn_ref[...]), 0.0)
    h_ref[...] = jnp.concatenate([hp, mod(xm_ref[...]), hn], axis=0).astype(BF16)

    nt8, nm8 = (tm + 2 * hl) // 8, tm // 8
    y, k0 = None, 0
    up = lambda j: jnp.dot(h_ref[...], wu_ref[j], preferred_element_type=F32).reshape(nt8, 8, 2 * FFN_CHUNK)
    u_next = up(0)
    for j in range(FFN_NCHUNK):
        u3 = u_next
        if j + 1 < FFN_NCHUNK:
            u_next = up(j + 1)
        w = cw_ref[j]
        cv = cb_ref[j]
        for kk in range(3):
            cv = cv + _shift_tiles(u3, kk - 1, 1, nm8) * w[kk:kk + 1, :]
        cv = cv.reshape(tm, 2 * FFN_CHUNK)
        act_ref[:, j * FFN_CHUNK:(j + 1) * FFN_CHUNK] = (_silu(cv[:, :FFN_CHUNK]) * cv[:, FFN_CHUNK:]).astype(BF16)
        if (j + 1) % FFN_DOWN_EVERY == 0 or j == FFN_NCHUNK - 1:
            k1 = (j + 1) * FFN_CHUNK
            part = jnp.dot(act_ref[:, k0:k1], wd_ref[k0:k1, :], preferred_element_type=F32)
            y = part if y is None else y + part
            k0 = k1
    out = xm_ref[...] + mod_ref[5:6, :] * y
    if final:
        out = _rms(out) * fg_ref[...]
    o_ref[...] = out


def _ffn(x, mod, layer, row, g, p, final_g=None):
    wu, cw, cb, wd = p
    nb, t, _ = x.shape
    tm = _tile(t, TOKEN_TILE)
    nt = t // tm
    hl = FFN_HALO
    r = tm // hl
    final = final_g is not None
    in_specs = [pl.BlockSpec((None, tm, D), lambda b, i: (b, i, 0)),
                pl.BlockSpec((None, hl, D), lambda b, i: (b, jnp.maximum(i * r - 1, 0), 0)),
                pl.BlockSpec((None, hl, D), lambda b, i: (b, jnp.minimum((i + 1) * r, t // hl - 1), 0)),
                _mod_spec(layer, row), _const_spec((1, D)),
                _const_spec((FFN_NCHUNK, D, 2 * FFN_CHUNK)), _const_spec((FFN_NCHUNK, 3, 2 * FFN_CHUNK)),
                _const_spec((FFN_NCHUNK, 1, 2 * FFN_CHUNK)), _const_spec((FFN_HIDDEN, D))]
    args = [x, x, x, mod, g, wu, cw, cb, wd]
    if final:
        in_specs.append(_const_spec((1, D)))
        args.append(final_g)
    return pl.pallas_call(
        functools.partial(_ffn_kernel, tm=tm, nt=nt, final=final),
        grid=(nb, nt),
        in_specs=in_specs,
        out_specs=pl.BlockSpec((None, tm, D), lambda b, i: (b, i, 0)),
        out_shape=jax.ShapeDtypeStruct((nb, t, D), F32),
        scratch_shapes=[pltpu.VMEM((tm + 2 * hl, D), BF16), pltpu.VMEM((tm, FFN_HIDDEN), BF16)],
        compiler_params=_params(("parallel", "parallel")),
    )(*args)


def _ffn_weights(w_up, conv_w, conv_b, w_down):
    def pair(a):
        lead = a.shape[:-1]
        a = a.reshape(lead + (2, FFN_NCHUNK, FFN_CHUNK))
        a = jnp.moveaxis(a, -2, 0)
        return a.reshape((FFN_NCHUNK,) + lead + (2 * FFN_CHUNK,))
    return (pair(w_up).astype(BF16), pair(conv_w), pair(conv_b[None, :]),
            w_down.astype(BF16))


def kernel(x, c, ctx, c_ctx, mod_w, mod_b, norm1_g, norm2_g, ffn_w_up, ffn_conv_w, ffn_conv_b, ffn_w_down,
           ssm_w_in, ssm_conv_w, ssm_conv_b, ssm_dt_bias, ssm_a_log, ssm_d, ssm_norm_g, ssm_w_out, attn_w_in,
           attn_lambda, attn_norm_g, attn_w_out, conf_w_pw1, conf_b_pw1, conf_dw_w, conf_dw_b, conf_ln_g,
           conf_ln_b, conf_w_pw2, conf_b_pw2, final_g):
    depth = mod_w.shape[0]
    nb = x.shape[0]
    mod = _mod_table(c, c_ctx, mod_w, mod_b)
    xc = ctx
    row = lambda a: a.reshape(1, -1)
    pad_dt = lambda a: jnp.pad(a.reshape(1, -1), ((0, 0), (0, DT_PAD - 2 * SSM_HEADS)))
    for i in range(depth):
        kind, j = i % N_MIXERS, i // N_MIXERS
        need_ctx = i < depth - 1
        g1 = row(norm1_g[i])
        if kind == 0:
            w_in = ssm_w_in[j]
            o_x, o_dt = D_INNER, D_INNER + SSM_CONV_CH
            wd = jnp.pad(w_in[:, o_dt:], ((0, 0), (0, DT_PAD - 2 * SSM_HEADS)))
            npass = SSM_CONV_CH // MAMBA_COLS
            wxz = jnp.concatenate([w_in[:, o_x:o_dt].reshape(D, npass, MAMBA_COLS),
                                   w_in[:, :o_x].reshape(D, npass, MAMBA_ZCOLS)], axis=-1)
            p = (jnp.moveaxis(wxz, 1, 0).astype(BF16), wd.astype(BF16),
                 ssm_conv_w[j], ssm_conv_b[j][:, None, :], pad_dt(ssm_dt_bias[j]), pad_dt(ssm_a_log[j]),
                 jnp.repeat(ssm_d[j], SSM_HEAD_DIM, axis=-1)[:, None, :], row(ssm_norm_g[j]),
                 ssm_w_out[j].astype(BF16))
            x, xc = _mamba_layer(x, xc, mod, i, nb, g1, p, need_ctx)
        elif kind == 1:
            w_in = attn_w_in[j].astype(BF16)
            p = (w_in[:, :D], w_in[:, D:2 * D], w_in[:, 2 * D:], attn_lambda[j], row(attn_norm_g[j]),
                 attn_w_out[j].astype(BF16))
            x, xc = _attn_layer(x, xc, mod, i, nb, g1, p, need_ctx)
        else:
            p = (conf_w_pw1[j].astype(BF16), row(conf_b_pw1[j]), conf_dw_w[j], row(conf_dw_b[j]),
                 row(conf_ln_g[j]), row(conf_ln_b[j]), conf_w_pw2[j].astype(BF16), row(conf_b_pw2[j]))
            x = _conformer(x, mod, i, None, g1, p)
            if need_ctx:
                xc = _conformer(xc, mod, i, nb, g1, p)
        fp = _ffn_weights(ffn_w_up[i], ffn_conv_w[i], ffn_conv_b[i], ffn_w_down[i])
        g2 = row(norm2_g[i])
        x = _ffn(x, mod, i, None, g2, fp, final_g=row(final_g) if i == depth - 1 else None)
        if need_ctx:
            xc = _ffn(xc, mod, i, nb, g2, fp)
    return x
```

```python
import functools
import math

import jax
import jax.numpy as jnp
from jax import lax
from jax.experimental import pallas as pl
from jax.experimental.pallas import tpu as pltpu

F32 = jnp.float32
BF16 = jnp.bfloat16
HIGHEST = lax.Precision.HIGHEST

D = 1024
N_MOD = 6
N_MIXERS = 3
EPS = 1e-6
GRID_W = 64
ROPE_THETA = 10000.0

SSM_HEAD_DIM = 64
SSM_HEADS = 32
SSM_GROUPS = 8
SSM_HPG = SSM_HEADS // SSM_GROUPS
SSM_STATE = 128
SSM_CONV = 4
SSM_CHUNK = 128
D_INNER = SSM_HEADS * SSM_HEAD_DIM
SSM_GW = SSM_HPG * SSM_HEAD_DIM
SSM_BC = SSM_GROUPS * SSM_STATE
SSM_CONV_CH = D_INNER + 2 * SSM_BC
DT_PAD = 128
SSD_HALO = 16

DA_HEAD_DIM = 64
DA_HEADS = D // (2 * DA_HEAD_DIM)
DA_V = 2 * DA_HEAD_DIM

CONF_KERNEL = 31
CONF_HALO = 16

FFN_HIDDEN = 2816
FFN_CHUNK = 256
FFN_NCHUNK = FFN_HIDDEN // FFN_CHUNK
FFN_HALO = 8
FFN_DOWN_EVERY = 4

VMEM_LIMIT = 56 * 1024 * 1024
TOKEN_TILE = 512
ATTN_Q_TILE = 512


def _sigmoid(x):
    return 1.0 / (1.0 + jnp.exp(-x))


def _silu(x):
    return x * _sigmoid(x)


def _softplus(x):
    return jnp.maximum(x, 0.0) + jnp.log(1.0 + jnp.exp(-jnp.abs(x)))


def _rms(x):
    return x * lax.rsqrt(jnp.mean(x * x, axis=-1, keepdims=True) + EPS)


def _shift_tiles(x3, s, first, count):
    if s == 0:
        return x3[first:first + count]
    sub = lax.broadcasted_iota(jnp.int32, (count,) + x3.shape[1:], 1)
    rk = pltpu.roll(x3, (-s) % 8, 1)
    if s < 0:
        return jnp.where(sub >= -s, rk[first:first + count], rk[first - 1:first - 1 + count])
    return jnp.where(sub < 8 - s, rk[first:first + count], rk[first + 1:first + 1 + count])


def _dot01(m01, x, *, left):
    out, rest = None, x
    for _ in range(3):
        piece = rest.astype(BF16)
        part = jnp.dot(m01, piece, preferred_element_type=F32) if left else \
            jnp.dot(piece, m01, preferred_element_type=F32)
        out = part if out is None else out + part
        rest = rest - piece.astype(F32)
    return out


def _params(sem, vmem=VMEM_LIMIT):
    return pltpu.CompilerParams(dimension_semantics=sem, vmem_limit_bytes=vmem)


def _mod_spec(layer, row):
    if row is None:
        return pl.BlockSpec((None, None, N_MOD, D), lambda b, *_: (layer, b, 0, 0))
    return pl.BlockSpec((None, None, N_MOD, D), lambda b, *_: (layer, row, 0, 0))


def _const_spec(shape):
    nd = len(shape)
    return pl.BlockSpec(shape, lambda *_: (0,) * nd)


def _tile(n, pref):
    return pref if n % pref == 0 else n


def _mod_kernel(s_ref, w_ref, b_ref, o_ref):
    s = _silu(s_ref[...])
    o_ref[...] = jnp.dot(s, w_ref[...], preferred_element_type=F32, precision=HIGHEST) + b_ref[...]


def _mod_table(c, c_ctx, mod_w, mod_b):
    depth = mod_w.shape[0]
    nb = c.shape[0]
    rows = -(-(nb + 1) // 8) * 8
    s = jnp.zeros((rows, D), F32).at[:nb].set(c).at[nb].set(c_ctx)
    tn = 1536
    out = pl.pallas_call(
        _mod_kernel,
        grid=(depth, N_MOD * D // tn),
        in_specs=[pl.BlockSpec((rows, D), lambda i, n: (0, 0)),
                  pl.BlockSpec((None, D, tn), lambda i, n: (i, 0, n)),
                  pl.BlockSpec((None, 1, tn), lambda i, n: (i, 0, n))],
        out_specs=pl.BlockSpec((None, rows, tn), lambda i, n: (i, 0, n)),
        out_shape=jax.ShapeDtypeStruct((depth, rows, N_MOD * D), F32),
        compiler_params=_params(("arbitrary", "arbitrary")),
    )(s, mod_w, mod_b.reshape(depth, 1, N_MOD * D))
    return out.reshape(depth, rows, N_MOD, D)


def _mamba_in_kernel(x_ref, mod_ref, g_ref, wz_ref, wx_ref, wd_ref, z_ref, xbc_ref, dt_ref):
    gs = g_ref[...] * (1.0 + mod_ref[1:2, :])
    h = (_rms(x_ref[...]) * gs + mod_ref[0:1, :]).astype(BF16)
    for n in range(0, D_INNER, 1024):
        z_ref[:, n:n + 1024] = jnp.dot(h, wz_ref[:, n:n + 1024], preferred_element_type=F32).astype(z_ref.dtype)
    for n in range(0, SSM_CONV_CH, 1024):
        xbc_ref[:, n:n + 1024] = jnp.dot(h, wx_ref[:, n:n + 1024],
                                         preferred_element_type=F32).astype(xbc_ref.dtype)
    dt_ref[...] = jnp.dot(h, wd_ref[...], preferred_element_type=F32)


def _mamba_in(x, mod, layer, row, g, wz, wx, wd):
    nb, t, _ = x.shape
    tm = _tile(t, TOKEN_TILE)
    tok = lambda w: pl.BlockSpec((None, tm, w), lambda b, i: (b, i, 0))
    return pl.pallas_call(
        _mamba_in_kernel,
        grid=(nb, t // tm),
        in_specs=[tok(D), _mod_spec(layer, row), _const_spec((1, D)),
                  _const_spec((D, D_INNER)), _const_spec((D, SSM_CONV_CH)), _const_spec((D, DT_PAD))],
        out_specs=[tok(D_INNER), tok(SSM_CONV_CH), tok(DT_PAD)],
        out_shape=[jax.ShapeDtypeStruct((nb, t, D_INNER), BF16),
                   jax.ShapeDtypeStruct((nb, t, SSM_CONV_CH), BF16),
                   jax.ShapeDtypeStruct((nb, t, DT_PAD), F32)],
        compiler_params=_params(("parallel", "parallel")),
    )(x, mod, g, wz, wx, wd)


def _ssd_kernel(*refs, rev, nc, write_y, combine, conv):
    it = iter(refs)
    u_ref = next(it)
    if conv:
        xh_ref, cw_ref, cbias_ref = (next(it) for _ in range(3))
    dt_ref, dtb_ref, alog_ref, dsk_ref, e_ref, h0_ref = (next(it) for _ in range(6))
    if combine:
        yf_ref = next(it)
    y_ref = next(it) if write_y else None
    hfin_ref = next(it)
    ht_ref = next(it)

    k = pl.program_id(1)
    ch = SSM_CHUNK
    off = SSM_HEADS if rev else 0

    @pl.when(k == 0)
    def _():
        ht_ref[...] = h0_ref[...]

    if conv:
        c = (nc - 1 - k) if rev else k
        side_on = (c < nc - 1) if rev else (c > 0)

        def load(c0, w):
            main = u_ref[:, c0:c0 + w].astype(F32)
            side = jnp.where(side_on, xh_ref[:, c0:c0 + w].astype(F32), 0.0)
            buf = jnp.concatenate([main, side] if rev else [side, main], axis=0)
            x3 = buf.reshape((ch + SSD_HALO) // 8, 8, w)
            first = 0 if rev else SSD_HALO // 8
            acc = cbias_ref[:, c0:c0 + w]
            for kk in range(SSM_CONV):
                s = (SSM_CONV - 1 - kk) * (1 if rev else -1)
                acc = acc + _shift_tiles(x3, s, first, ch // 8) * cw_ref[kk:kk + 1, c0:c0 + w]
            return _silu(acc).reshape(ch, w)
    else:
        def load(c0, w):
            return u_ref[:, c0:c0 + w]

    dtv = _softplus(dt_ref[...] + dtb_ref[...])
    la = dtv * (-jnp.exp(alog_ref[...]))
    ri = lax.broadcasted_iota(jnp.int32, (ch, ch), 0)
    ci = lax.broadcasted_iota(jnp.int32, (ch, ch), 1)
    keep = (ci >= ri) if rev else (ci <= ri)
    acum = _dot01(jnp.where(keep, 1.0, 0.0).astype(BF16), la, left=True)
    last = 0 if rev else ch - 1
    total = acum[last:last + 1, :]
    wcol = jnp.exp(total - acum) * dtv
    offv = jnp.exp(acum)
    acum2 = acum * math.log2(math.e)
    acum2_t = acum2.T
    dt_t = dtv.T

    lane = lax.broadcasted_iota(jnp.int32, (ch, SSM_GW), 1)

    wexp = jnp.dot(wcol.astype(BF16), e_ref[...], preferred_element_type=F32)
    sexp = jnp.dot(offv.astype(BF16), e_ref[...], preferred_element_type=F32)
    cdec = _dot01(e_ref[...], jnp.broadcast_to(offv[last:last + 1, :], (8, DT_PAD)), left=False)[0:1, :]

    for g in range(SSM_GROUPS):
        gc = slice(g * SSM_GW, (g + 1) * SSM_GW)
        xg = load(g * SSM_GW, SSM_GW)
        xg16, xg = xg.astype(BF16), xg.astype(F32)
        bg16 = load(D_INNER + g * SSM_STATE, SSM_STATE).astype(BF16)
        xw = (xg * wexp[:, gc]).astype(BF16)
        st_t = lax.dot_general(bg16, xw, (((0,), (0,)), ((), ())), preferred_element_type=F32)
        sg = sexp[:, gc]
        ht_old = ht_ref[g]
        if write_y:
            cg16 = load(D_INNER + SSM_BC + g * SSM_STATE, SSM_STATE).astype(BF16)
            cbm = lax.dot_general(cg16, bg16, (((1,), (1,)), ((), ())), preferred_element_type=F32)
            yoff = jnp.dot(cg16, ht_old.astype(BF16), preferred_element_type=F32)
            ms, xs = [], []
            for j in range(SSM_HPG):
                col = off + SSM_HPG * g + j
                seg = acum2[:, col:col + 1] - acum2_t[col:col + 1, :]
                lmat = jnp.exp2(jnp.where(keep, seg, -jnp.inf))
                ms.append((cbm * lmat * dt_t[col:col + 1, :]).astype(BF16))
                in_head = (lane >= j * SSM_HEAD_DIM) & (lane < (j + 1) * SSM_HEAD_DIM)
                xs.append(jnp.where(in_head, xg16, jnp.zeros_like(xg16)))
            ydiag = jnp.dot(jnp.concatenate(ms, axis=1), jnp.concatenate(xs, axis=0),
                            preferred_element_type=F32)
            y = ydiag + yoff * sg + dsk_ref[:, g * SSM_GW:(g + 1) * SSM_GW] * xg
            cols = slice(g * SSM_GW, (g + 1) * SSM_GW)
            if combine:
                y = y + yf_ref[:, cols].astype(F32)
            y_ref[:, cols] = y.astype(y_ref.dtype)
        ht_ref[g] = ht_old * cdec[:, gc] + st_t

    @pl.when(k == nc - 1)
    def _():
        hfin_ref[...] = ht_ref[...]


def _ssd(u, dt, h0, dt_bias, a_log, d_skip, *, rev, write_y, taps=None, combine=None, y_dtype=F32):
    nb, t, _ = u.shape
    nc = t // SSM_CHUNK
    hb = SSM_CHUNK // SSD_HALO
    if rev:
        cidx = lambda k: nc - 1 - k
        hidx = lambda k: jnp.minimum((nc - k) * hb, t // SSD_HALO - 1)
    else:
        cidx = lambda k: k
        hidx = lambda k: jnp.maximum(k * hb - 1, 0)
    chunk = lambda w: pl.BlockSpec((None, SSM_CHUNK, w), lambda b, k: (b, cidx(k), 0))
    state = pl.BlockSpec((None, SSM_GROUPS, SSM_STATE, SSM_GW), lambda b, k: (b, 0, 0, 0))
    in_specs, args = [chunk(SSM_CONV_CH)], [u]
    scratch = [pltpu.VMEM((SSM_GROUPS, SSM_STATE, SSM_GW), F32)]
    if taps is not None:
        in_specs += [pl.BlockSpec((None, SSD_HALO, SSM_CONV_CH), lambda b, k: (b, hidx(k), 0)),
                     _const_spec((SSM_CONV, SSM_CONV_CH)), _const_spec((1, SSM_CONV_CH))]
        args += [u, taps[0], taps[1]]
    off = SSM_HEADS if rev else 0
    spread = jnp.repeat(jnp.eye(DT_PAD, dtype=BF16)[:, off:off + SSM_HEADS], SSM_HEAD_DIM, axis=1)
    in_specs += [chunk(DT_PAD), _const_spec((1, DT_PAD)), _const_spec((1, DT_PAD)), _const_spec((1, D_INNER)),
                 _const_spec((DT_PAD, D_INNER)), state]
    args += [dt, dt_bias, a_log, d_skip, spread, h0]
    out_specs, out_shape = [], []
    if combine is not None:
        in_specs.append(chunk(D_INNER))
        args.append(combine)
    if write_y:
        out_specs.append(chunk(D_INNER))
        out_shape.append(jax.ShapeDtypeStruct((nb, t, D_INNER), y_dtype))
    out_specs.append(state)
    out_shape.append(jax.ShapeDtypeStruct((nb, SSM_GROUPS, SSM_STATE, SSM_GW), F32))
    outs = pl.pallas_call(
        functools.partial(_ssd_kernel, rev=rev, nc=nc, write_y=write_y, combine=combine is not None,
                          conv=taps is not None),
        grid=(nb, nc),
        in_specs=in_specs, out_specs=out_specs, out_shape=out_shape,
        scratch_shapes=scratch,
        compiler_params=_params(("parallel", "arbitrary")),
    )(*args)
    return outs if write_y else (None, outs[0])


def _resid_kernel(*refs, gate_row, ssm_gate):
    if ssm_gate:
        a_ref, z_ref, ng_ref, w_ref, b_ref, x_ref, mod_ref, o_ref = refs
        parts = []
        for g in range(SSM_GROUPS):
            cols = slice(g * SSM_GW, (g + 1) * SSM_GW)
            yg = a_ref[:, cols].astype(F32) * _silu(z_ref[:, cols].astype(F32))
            parts.append((_rms(yg) * ng_ref[:, cols]).astype(BF16))
        a = jnp.concatenate(parts, axis=1)
    else:
        a_ref, w_ref, b_ref, x_ref, mod_ref, o_ref = refs
        a = a_ref[...]
    y = jnp.dot(a, w_ref[...], preferred_element_type=F32) + b_ref[...]
    o_ref[...] = x_ref[...] + mod_ref[gate_row:gate_row + 1, :] * y


def _resid_matmul(a, w, bias, x, mod, layer, row, gate_row, ssm_gate=None):
    nb, t, kdim = a.shape
    tm = _tile(t, TOKEN_TILE)
    tok = lambda wd: pl.BlockSpec((None, tm, wd), lambda b, i: (b, i, 0))
    in_specs, args = [tok(kdim)], [a]
    if ssm_gate is not None:
        in_specs += [tok(kdim), _const_spec((1, kdim))]
        args += list(ssm_gate)
    in_specs += [_const_spec((kdim, D)), _const_spec((1, D)), tok(D), _mod_spec(layer, row)]
    args += [w, bias, x, mod]
    return pl.pallas_call(
        functools.partial(_resid_kernel, gate_row=gate_row, ssm_gate=ssm_gate is not None),
        grid=(nb, t // tm),
        in_specs=in_specs,
        out_specs=tok(D),
        out_shape=jax.ShapeDtypeStruct((nb, t, D), F32),
        compiler_params=_params(("parallel", "parallel")),
    )(*args)


def _mamba_layer(x, xc, mod, layer, nb, g1, p, need_ctx):
    wz, wx, wd, cw, cbias, dtb, alog, dsk, ng, wout = p
    z_l, xbc_l, dt_l = _mamba_in(x, mod, layer, None, g1, wz, wx, wd)
    z_c, xbc_c, dt_c = _mamba_in(xc, mod, layer, nb, g1, wz, wx, wd)
    h0 = jnp.zeros((x.shape[0], SSM_GROUPS, SSM_STATE, SSM_GW), F32)
    dirp = lambda d: (dtb, alog, dsk[d])
    taps = lambda d: (cw[d], cbias[d])
    yf_c, hf = _ssd(xbc_c, dt_c, h0, *dirp(0), rev=False, write_y=need_ctx, taps=taps(0))
    yf_l, _ = _ssd(xbc_l, dt_l, hf, *dirp(0), rev=False, write_y=True, taps=taps(0))
    y_c, hb = _ssd(xbc_c, dt_c, h0, *dirp(1), rev=True, write_y=need_ctx, taps=taps(1),
                   combine=yf_c if need_ctx else None, y_dtype=BF16)
    y_l, _ = _ssd(xbc_l, dt_l, hb, *dirp(1), rev=True, write_y=True, taps=taps(1), combine=yf_l, y_dtype=BF16)
    zero_b = jnp.zeros((1, D), F32)
    x = _resid_matmul(y_l, wout, zero_b, x, mod, layer, None, 2, ssm_gate=(z_l, ng))
    if need_ctx:
        xc = _resid_matmul(y_c, wout, zero_b, xc, mod, layer, nb, 2, ssm_gate=(z_c, ng))
    return x, xc


def _qkv_kernel(*refs, rope):
    if rope:
        x_ref, mod_ref, g_ref, wq_ref, wk_ref, wv_ref, cos_ref, sin_ref, q_ref, k_ref, v_ref = refs
    else:
        x_ref, mod_ref, g_ref, wq_ref, wk_ref, wv_ref, q_ref, k_ref, v_ref = refs
    gs = g_ref[...] * (1.0 + mod_ref[1:2, :])
    h = (_rms(x_ref[...]) * gs + mod_ref[0:1, :]).astype(BF16)
    tm = h.shape[0]
    if rope:
        cos, sin = cos_ref[...], sin_ref[...]
        lane = lax.broadcasted_iota(jnp.int32, (tm, 128), 1)
        first = (lane % 32) < 16

    def rot(a_ref, w_ref, scale):
        for j in range(D // 128):
            blk = jnp.dot(h, w_ref[:, j * 128:(j + 1) * 128], preferred_element_type=F32)
            if scale != 1.0:
                blk = blk * scale
            if rope:
                partner = jnp.where(first, pltpu.roll(blk, 112, 1), pltpu.roll(blk, 16, 1))
                blk = blk * cos + partner * sin
            a_ref[:, j * 128:(j + 1) * 128] = blk.astype(a_ref.dtype)

    rot(q_ref, wq_ref, DA_HEAD_DIM ** -0.5 * math.log2(math.e))
    rot(k_ref, wk_ref, 1.0)
    v_ref[...] = jnp.dot(h, wv_ref[...], preferred_element_type=F32).astype(v_ref.dtype)


def _rope_tables(s):
    lane = jnp.arange(128)
    pos = jnp.arange(s)
    coord = jnp.where(((lane % 64) // 32 == 0)[None, :], (pos // GRID_W)[:, None], (pos % GRID_W)[:, None])
    n_freq = DA_HEAD_DIM // 4
    inv = ROPE_THETA ** (-(lane % n_freq).astype(F32) / n_freq)
    ang = coord.astype(F32) * inv[None, :]
    sign = jnp.where((lane % 32) < 16, -1.0, 1.0)
    return jnp.cos(ang), jnp.sin(ang) * sign[None, :]


def _qkv(x, mod, layer, row, g, wq, wk, wv, tables):
    nb, t, _ = x.shape
    tm = _tile(t, TOKEN_TILE)
    tok = lambda: pl.BlockSpec((None, tm, D), lambda b, i: (b, i, 0))
    in_specs = [tok(), _mod_spec(layer, row), _const_spec((1, D))] + [_const_spec((D, D))] * 3
    args = [x, mod, g, wq, wk, wv]
    if tables is not None:
        in_specs += [pl.BlockSpec((tm, 128), lambda b, i: (i, 0))] * 2
        args += list(tables)
    return pl.pallas_call(
        functools.partial(_qkv_kernel, rope=tables is not None),
        grid=(nb, t // tm),
        in_specs=in_specs, out_specs=[tok()] * 3,
        out_shape=[jax.ShapeDtypeStruct((nb, t, D), BF16)] * 3,
        compiler_params=_params(("parallel", "parallel")),
    )(*args)


def _attn_kernel(*refs, nseg, lam_init):
    lam_ref, ng_ref, q_ref = refs[:3]
    kv = refs[3:3 + 2 * nseg]
    o_ref = refs[3 + 2 * nseg]
    q = q_ref[...]
    tq = q.shape[0]
    lane = lax.broadcasted_iota(jnp.int32, q.shape, 1)
    zero = jnp.zeros_like(q)
    qs = [jnp.where((lane >= e * DA_HEAD_DIM) & (lane < (e + 1) * DA_HEAD_DIM), q, zero) for e in range(2)]
    scores = [[lax.dot_general(qe, kv[2 * i][...], (((1,), (1,)), ((), ())), preferred_element_type=F32)
               for i in range(nseg)] for qe in qs]
    vs = [jnp.concatenate([kv[2 * i + 1][...], jnp.ones(kv[2 * i + 1].shape, BF16)], axis=1) for i in range(nseg)]
    halves = []
    for e in range(2):
        m = scores[e][0].max(axis=-1, keepdims=True)
        for s in scores[e][1:]:
            m = jnp.maximum(m, s.max(axis=-1, keepdims=True))
        acc = jnp.zeros((tq, 2 * DA_V), F32)
        for i in range(nseg):
            p = jnp.exp2(scores[e][i] - m)
            acc = acc + jnp.dot(p.astype(BF16), vs[i], preferred_element_type=F32)
        halves.append(acc[:, :DA_V] / acc[:, DA_V:DA_V + 1])
    lp = lam_ref[...]
    lam = (jnp.exp(jnp.sum(lp[0:1] * lp[1:2], axis=-1, keepdims=True))
           - jnp.exp(jnp.sum(lp[2:3] * lp[3:4], axis=-1, keepdims=True)) + lam_init)
    o = halves[0] - lam * halves[1]
    o_ref[...] = (_rms(o) * ng_ref[...] * (1.0 - lam_init)).astype(o_ref.dtype)


def _attention(q, segs, lam_p, ng, lam_init):
    nb, t, _ = q.shape
    tq = _tile(t, ATTN_Q_TILE)
    in_specs = [_const_spec((4, DA_HEAD_DIM)), _const_spec((1, DA_V)),
                pl.BlockSpec((None, tq, DA_V), lambda b, h, i: (b, i, h))]
    args = [lam_p, ng, q]
    for k, v in segs:
        tk = k.shape[1]
        in_specs += [pl.BlockSpec((None, tk, DA_V), lambda b, h, i: (b, 0, h))] * 2
        args += [k, v]
    return pl.pallas_call(
        functools.partial(_attn_kernel, nseg=len(segs), lam_init=lam_init),
        grid=(nb, DA_HEADS, t // tq),
        in_specs=in_specs,
        out_specs=pl.BlockSpec((None, tq, DA_V), lambda b, h, i: (b, i, h)),
        out_shape=jax.ShapeDtypeStruct((nb, t, D), BF16),
        compiler_params=_params(("parallel", "parallel", "arbitrary")),
    )(*args)


def _attn_layer(x, xc, mod, layer, nb, g1, p, need_ctx):
    wq, wk, wv, lam_p, ng, wout = p
    lam_init = 0.8 - 0.6 * math.exp(-0.3 * layer)
    q_l, k_l, v_l = _qkv(x, mod, layer, None, g1, wq, wk, wv, _rope_tables(x.shape[1]))
    q_c, k_c, v_c = _qkv(xc, mod, layer, nb, g1, wq, wk, wv, None)
    zero_b = jnp.zeros((1, D), F32)
    o_l = _attention(q_l, [(k_c, v_c), (k_l, v_l)], lam_p, ng, lam_init)
    x = _resid_matmul(o_l, wout, zero_b, x, mod, layer, None, 2)
    if need_ctx:
        o_c = _attention(q_c, [(k_c, v_c)], lam_p, ng, lam_init)
        xc = _resid_matmul(o_c, wout, zero_b, xc, mod, layer, nb, 2)
    return x, xc


def _conf_kernel(xm_ref, xp_ref, xn_ref, mod_ref, g_ref, w1_ref, b1_ref, dw_ref, db_ref, lg_ref, lb_ref,
                 w2_ref, b2_ref, o_ref, u_ref, c_ref, sh_ref, *, tm, nt):
    t = pl.program_id(1)
    hl = CONF_HALO
    gs = g_ref[...] * (1.0 + mod_ref[1:2, :])
    shift = mod_ref[0:1, :]

    def glu(x, valid):
        h = (_rms(x) * gs + shift).astype(BF16)
        a = jnp.dot(h, w1_ref[...], preferred_element_type=F32) + b1_ref[...]
        u = a[:, :D] * _sigmoid(a[:, D:])
        return u if valid is None else jnp.where(valid, u, 0.0)

    u_ref[0:hl, :] = glu(xp_ref[...], t > 0)
    u_ref[hl:hl + tm, :] = glu(xm_ref[...], None)
    u_ref[hl + tm:, :] = glu(xn_ref[...], t < nt - 1)

    rb, cw = 64, 256
    pad = CONF_KERNEL // 2
    for c0 in range(0, D, cw):
        blk = u_ref[:, c0:c0 + cw]
        for r in range(1, 8):
            sh_ref[r - 1, :, c0:c0 + cw] = pltpu.roll(blk, r, 0)
    for r0 in range(0, tm, rb):
        for c0 in range(0, D, cw):
            acc = jnp.broadcast_to(db_ref[:, c0:c0 + cw], (rb, cw))
            for kk in range(CONF_KERNEL):
                d = kk - pad
                r = (-d) % 8
                s = hl + d + r + r0
                src = u_ref if r == 0 else sh_ref.at[r - 1]
                acc = acc + src[s:s + rb, c0:c0 + cw] * dw_ref[kk:kk + 1, c0:c0 + cw]
            c_ref[r0:r0 + rb, c0:c0 + cw] = acc

    cv = c_ref[...]
    mu = jnp.mean(cv, axis=-1, keepdims=True)
    xc = cv - mu
    ln = xc * lax.rsqrt(jnp.mean(xc * xc, axis=-1, keepdims=True) + EPS) * lg_ref[...] + lb_ref[...]
    y = jnp.dot(_silu(ln).astype(BF16), w2_ref[...], preferred_element_type=F32) + b2_ref[...]
    o_ref[...] = xm_ref[...] + mod_ref[2:3, :] * y


def _conformer(x, mod, layer, row, g, p):
    w1, b1, dw, db, lg, lb, w2, b2 = p
    nb, t, _ = x.shape
    tm = _tile(t, 256)
    nt = t // tm
    hl = CONF_HALO
    r = tm // hl
    return pl.pallas_call(
        functools.partial(_conf_kernel, tm=tm, nt=nt),
        grid=(nb, nt),
        in_specs=[pl.BlockSpec((None, tm, D), lambda b, i: (b, i, 0)),
                  pl.BlockSpec((None, hl, D), lambda b, i: (b, jnp.maximum(i * r - 1, 0), 0)),
                  pl.BlockSpec((None, hl, D), lambda b, i: (b, jnp.minimum((i + 1) * r, t // hl - 1), 0)),
                  _mod_spec(layer, row), _const_spec((1, D)),
                  _const_spec((D, 2 * D)), _const_spec((1, 2 * D)),
                  _const_spec((CONF_KERNEL, D)), _const_spec((1, D)), _const_spec((1, D)), _const_spec((1, D)),
                  _const_spec((D, D)), _const_spec((1, D))],
        out_specs=pl.BlockSpec((None, tm, D), lambda b, i: (b, i, 0)),
        out_shape=jax.ShapeDtypeStruct((nb, t, D), F32),
        scratch_shapes=[pltpu.VMEM((tm + 2 * hl, D), F32), pltpu.VMEM((tm, D), F32),
                        pltpu.VMEM((7, tm + 2 * hl, D), F32)],
        compiler_params=_params(("parallel", "parallel")),
    )(x, x, x, mod, g, w1, b1, dw, db, lg, lb, w2, b2)


def _ffn_kernel(*refs, tm, nt, final):
    if final:
        (xm_ref, xp_ref, xn_ref, mod_ref, g_ref, wu_ref, cw_ref, cb_ref, wd_ref, fg_ref,
         o_ref, h_ref, act_ref) = refs
    else:
        (xm_ref, xp_ref, xn_ref, mod_ref, g_ref, wu_ref, cw_ref, cb_ref, wd_ref,
         o_ref, h_ref, act_ref) = refs
    t = pl.program_id(1)
    hl = FFN_HALO
    gs = g_ref[...] * (1.0 + mod_ref[4:5, :])
    shift = mod_ref[3:4, :]
    mod = lambda x: _rms(x) * gs + shift
    hp = jnp.where(t > 0, mod(xp_ref[...]), 0.0)
    hn = jnp.where(t < nt - 1, mod(xn_ref[...]), 0.0)
    h_ref[...] = jnp.concatenate([hp, mod(xm_ref[...]), hn], axis=0).astype(BF16)

    nt8, nm8 = (tm + 2 * hl) // 8, tm // 8
    y, k0 = None, 0
    up = lambda j: jnp.dot(h_ref[...], wu_ref[j], preferred_element_type=F32).reshape(nt8, 8, 2 * FFN_CHUNK)
    u_next = up(0)
    for j in range(FFN_NCHUNK):
        u3 = u_next
        if j + 1 < FFN_NCHUNK:
            u_next = up(j + 1)
        w = cw_ref[j]
        cv = cb_ref[j]
        for kk in range(3):
            cv = cv + _shift_tiles(u3, kk - 1, 1, nm8) * w[kk:kk + 1, :]
        cv = cv.reshape(tm, 2 * FFN_CHUNK)
        act_ref[:, j * FFN_CHUNK:(j + 1) * FFN_CHUNK] = (_silu(cv[:, :FFN_CHUNK]) * cv[:, FFN_CHUNK:]).astype(BF16)
        if (j + 1) % FFN_DOWN_EVERY == 0 or j == FFN_NCHUNK - 1:
            k1 = (j + 1) * FFN_CHUNK
            part = jnp.dot(act_ref[:, k0:k1], wd_ref[k0:k1, :], preferred_element_type=F32)
            y = part if y is None else y + part
            k0 = k1
    out = xm_ref[...] + mod_ref[5:6, :] * y
    if final:
        out = _rms(out) * fg_ref[...]
    o_ref[...] = out


def _ffn(x, mod, layer, row, g, p, final_g=None):
    wu, cw, cb, wd = p
    nb, t, _ = x.shape
    tm = _tile(t, TOKEN_TILE)
    nt = t // tm
    hl = FFN_HALO
    r = tm // hl
    final = final_g is not None
    in_specs = [pl.BlockSpec((None, tm, D), lambda b, i: (b, i, 0)),
                pl.BlockSpec((None, hl, D), lambda b, i: (b, jnp.maximum(i * r - 1, 0), 0)),
                pl.BlockSpec((None, hl, D), lambda b, i: (b, jnp.minimum((i + 1) * r, t // hl - 1), 0)),
                _mod_spec(layer, row), _const_spec((1, D)),
                _const_spec((FFN_NCHUNK, D, 2 * FFN_CHUNK)), _const_spec((FFN_NCHUNK, 3, 2 * FFN_CHUNK)),
                _const_spec((FFN_NCHUNK, 1, 2 * FFN_CHUNK)), _const_spec((FFN_HIDDEN, D))]
    args = [x, x, x, mod, g, wu, cw, cb, wd]
    if final:
        in_specs.append(_const_spec((1, D)))
        args.append(final_g)
    return pl.pallas_call(
        functools.partial(_ffn_kernel, tm=tm, nt=nt, final=final),
        grid=(nb, nt),
        in_specs=in_specs,
        out_specs=pl.BlockSpec((None, tm, D), lambda b, i: (b, i, 0)),
        out_shape=jax.ShapeDtypeStruct((nb, t, D), F32),
        scratch_shapes=[pltpu.VMEM((tm + 2 * hl, D), BF16), pltpu.VMEM((tm, FFN_HIDDEN), BF16)],
        compiler_params=_params(("parallel", "parallel")),
    )(*args)


def _ffn_weights(w_up, conv_w, conv_b, w_down):
    def pair(a):
        lead = a.shape[:-1]
        a = a.reshape(lead + (2, FFN_NCHUNK, FFN_CHUNK))
        a = jnp.moveaxis(a, -2, 0)
        return a.reshape((FFN_NCHUNK,) + lead + (2 * FFN_CHUNK,))
    return (pair(w_up).astype(BF16), pair(conv_w), pair(conv_b[None, :]),
            w_down.astype(BF16))


def kernel(x, c, ctx, c_ctx, mod_w, mod_b, norm1_g, norm2_g, ffn_w_up, ffn_conv_w, ffn_conv_b, ffn_w_down,
           ssm_w_in, ssm_conv_w, ssm_conv_b, ssm_dt_bias, ssm_a_log, ssm_d, ssm_norm_g, ssm_w_out, attn_w_in,
           attn_lambda, attn_norm_g, attn_w_out, conf_w_pw1, conf_b_pw1, conf_dw_w, conf_dw_b, conf_ln_g,
           conf_ln_b, conf_w_pw2, conf_b_pw2, final_g):
    depth = mod_w.shape[0]
    nb = x.shape[0]
    mod = _mod_table(c, c_ctx, mod_w, mod_b)
    xc = ctx
    row = lambda a: a.reshape(1, -1)
    pad_dt = lambda a: jnp.pad(a.reshape(1, -1), ((0, 0), (0, DT_PAD - 2 * SSM_HEADS)))
    for i in range(depth):
        kind, j = i % N_MIXERS, i // N_MIXERS
        need_ctx = i < depth - 1
        g1 = row(norm1_g[i])
        if kind == 0:
            w_in = ssm_w_in[j]
            o_x, o_dt = D_INNER, D_INNER + SSM_CONV_CH
            wd = jnp.pad(w_in[:, o_dt:], ((0, 0), (0, DT_PAD - 2 * SSM_HEADS)))
            p = (w_in[:, :o_x].astype(BF16), w_in[:, o_x:o_dt].astype(BF16), wd.astype(BF16),
                 ssm_conv_w[j], ssm_conv_b[j][:, None, :], pad_dt(ssm_dt_bias[j]), pad_dt(ssm_a_log[j]),
                 jnp.repeat(ssm_d[j], SSM_HEAD_DIM, axis=-1)[:, None, :], row(ssm_norm_g[j]),
                 ssm_w_out[j].astype(BF16))
            x, xc = _mamba_layer(x, xc, mod, i, nb, g1, p, need_ctx)
        elif kind == 1:
            w_in = attn_w_in[j].astype(BF16)
            p = (w_in[:, :D], w_in[:, D:2 * D], w_in[:, 2 * D:], attn_lambda[j], row(attn_norm_g[j]),
                 attn_w_out[j].astype(BF16))
            x, xc = _attn_layer(x, xc, mod, i, nb, g1, p, need_ctx)
        else:
            p = (conf_w_pw1[j].astype(BF16), row(conf_b_pw1[j]), conf_dw_w[j], row(conf_dw_b[j]),
                 row(conf_ln_g[j]), row(conf_ln_b[j]), conf_w_pw2[j].astype(BF16), row(conf_b_pw2[j]))
            x = _conformer(x, mod, i, None, g1, p)
            if need_ctx:
                xc = _conformer(xc, mod, i, nb, g1, p)
        fp = _ffn_weights(ffn_w_up[i], ffn_conv_w[i], ffn_conv_b[i], ffn_w_down[i])
        g2 = row(norm2_g[i])
        x = _ffn(x, mod, i, None, g2, fp, final_g=row(final_g) if i == depth - 1 else None)
        if need_ctx:
            xc = _ffn(xc, mod, i, nb, g2, fp)
    return x
```

```python
import functools
import math

import jax
import jax.numpy as jnp
from jax import lax
from jax.experimental import pallas as pl
from jax.experimental.pallas import tpu as pltpu

F32 = jnp.float32
BF16 = jnp.bfloat16
HIGHEST = lax.Precision.HIGHEST

D = 1024
N_MOD = 6
N_MIXERS = 3
EPS = 1e-6
GRID_W = 64
ROPE_THETA = 10000.0

SSM_HEAD_DIM = 64
SSM_HEADS = 32
SSM_GROUPS = 8
SSM_HPG = SSM_HEADS // SSM_GROUPS
SSM_STATE = 128
SSM_CONV = 4
SSM_CHUNK = 128
D_INNER = SSM_HEADS * SSM_HEAD_DIM
SSM_GW = SSM_HPG * SSM_HEAD_DIM
SSM_BC = SSM_GROUPS * SSM_STATE
SSM_CONV_CH = D_INNER + 2 * SSM_BC
DT_PAD = 128
SSD_HALO = 16

DA_HEAD_DIM = 64
DA_HEADS = D // (2 * DA_HEAD_DIM)
DA_V = 2 * DA_HEAD_DIM

CONF_KERNEL = 31
CONF_HALO = 16
CONF_COLS = 256

FFN_HIDDEN = 2816
FFN_CHUNK = 256
FFN_NCHUNK = FFN_HIDDEN // FFN_CHUNK
FFN_HALO = 8
FFN_DOWN_EVERY = 4

VMEM_LIMIT = 56 * 1024 * 1024
TOKEN_TILE = 512
ATTN_Q_TILE = 512


def _sigmoid(x):
    return 1.0 / (1.0 + jnp.exp2(x * -math.log2(math.e)))


def _silu(x):
    return x * _sigmoid(x)


def _softplus(x):
    return jnp.maximum(x, 0.0) + jnp.log(1.0 + jnp.exp(-jnp.abs(x)))


def _rms(x):
    return x * lax.rsqrt(jnp.mean(x * x, axis=-1, keepdims=True) + EPS)


def _shift_tiles(x3, s, first, count):
    if s == 0:
        return x3[first:first + count]
    sub = lax.broadcasted_iota(jnp.int32, (count,) + x3.shape[1:], 1)
    rk = pltpu.roll(x3, (-s) % 8, 1)
    if s < 0:
        return jnp.where(sub >= -s, rk[first:first + count], rk[first - 1:first - 1 + count])
    return jnp.where(sub < 8 - s, rk[first:first + count], rk[first + 1:first + 1 + count])


def _dot01(m01, x, *, left):
    out, rest = None, x
    for _ in range(3):
        piece = rest.astype(BF16)
        part = jnp.dot(m01, piece, preferred_element_type=F32) if left else \
            jnp.dot(piece, m01, preferred_element_type=F32)
        out = part if out is None else out + part
        rest = rest - piece.astype(F32)
    return out


def _params(sem, vmem=VMEM_LIMIT):
    return pltpu.CompilerParams(dimension_semantics=sem, vmem_limit_bytes=vmem)


def _mod_spec(layer, row):
    if row is None:
        return pl.BlockSpec((None, None, N_MOD, D), lambda b, *_: (layer, b, 0, 0))
    return pl.BlockSpec((None, None, N_MOD, D), lambda b, *_: (layer, row, 0, 0))


def _const_spec(shape):
    nd = len(shape)
    return pl.BlockSpec(shape, lambda *_: (0,) * nd)


def _tile(n, pref):
    return pref if n % pref == 0 else n


def _mod_kernel(s_ref, w_ref, b_ref, o_ref):
    s = _silu(s_ref[...])
    o_ref[...] = jnp.dot(s, w_ref[...], preferred_element_type=F32, precision=HIGHEST) + b_ref[...]


def _mod_table(c, c_ctx, mod_w, mod_b):
    depth = mod_w.shape[0]
    nb = c.shape[0]
    rows = -(-(nb + 1) // 8) * 8
    s = jnp.zeros((rows, D), F32).at[:nb].set(c).at[nb].set(c_ctx)
    tn = 1536
    out = pl.pallas_call(
        _mod_kernel,
        grid=(depth, N_MOD * D // tn),
        in_specs=[pl.BlockSpec((rows, D), lambda i, n: (0, 0)),
                  pl.BlockSpec((None, D, tn), lambda i, n: (i, 0, n)),
                  pl.BlockSpec((None, 1, tn), lambda i, n: (i, 0, n))],
        out_specs=pl.BlockSpec((None, rows, tn), lambda i, n: (i, 0, n)),
        out_shape=jax.ShapeDtypeStruct((depth, rows, N_MOD * D), F32),
        compiler_params=_params(("arbitrary", "arbitrary")),
    )(s, mod_w, mod_b.reshape(depth, 1, N_MOD * D))
    return out.reshape(depth, rows, N_MOD, D)


def _mamba_in_kernel(x_ref, mod_ref, g_ref, wz_ref, wx_ref, wd_ref, z_ref, xbc_ref, dt_ref):
    gs = g_ref[...] * (1.0 + mod_ref[1:2, :])
    h = (_rms(x_ref[...]) * gs + mod_ref[0:1, :]).astype(BF16)
    for n in range(0, D_INNER, 1024):
        z_ref[:, n:n + 1024] = jnp.dot(h, wz_ref[:, n:n + 1024], preferred_element_type=F32).astype(z_ref.dtype)
    for n in range(0, SSM_CONV_CH, 1024):
        xbc_ref[:, n:n + 1024] = jnp.dot(h, wx_ref[:, n:n + 1024],
                                         preferred_element_type=F32).astype(xbc_ref.dtype)
    dt_ref[...] = jnp.dot(h, wd_ref[...], preferred_element_type=F32)


def _mamba_in(x, mod, layer, row, g, wz, wx, wd):
    nb, t, _ = x.shape
    tm = _tile(t, TOKEN_TILE)
    tok = lambda w: pl.BlockSpec((None, tm, w), lambda b, i: (b, i, 0))
    return pl.pallas_call(
        _mamba_in_kernel,
        grid=(nb, t // tm),
        in_specs=[tok(D), _mod_spec(layer, row), _const_spec((1, D)),
                  _const_spec((D, D_INNER)), _const_spec((D, SSM_CONV_CH)), _const_spec((D, DT_PAD))],
        out_specs=[tok(D_INNER), tok(SSM_CONV_CH), tok(DT_PAD)],
        out_shape=[jax.ShapeDtypeStruct((nb, t, D_INNER), BF16),
                   jax.ShapeDtypeStruct((nb, t, SSM_CONV_CH), BF16),
                   jax.ShapeDtypeStruct((nb, t, DT_PAD), F32)],
        compiler_params=_params(("parallel", "parallel")),
    )(x, mod, g, wz, wx, wd)


def _ssd_kernel(*refs, rev, nc, write_y, combine):
    it = iter(refs)
    u_ref, xh_ref, cw_ref, cbias_ref = (next(it) for _ in range(4))
    dt_ref, dtb_ref, alog_ref, dsk_ref, e_ref, h0_ref = (next(it) for _ in range(6))
    if combine:
        yf_ref = next(it)
    y_ref = next(it) if write_y else None
    hfin_ref = next(it)
    ht_ref = next(it)

    k = pl.program_id(1)
    ch = SSM_CHUNK
    off = SSM_HEADS if rev else 0

    @pl.when(k == 0)
    def _():
        ht_ref[...] = h0_ref[...]

    c = (nc - 1 - k) if rev else k
    side_on = (c < nc - 1) if rev else (c > 0)

    def load(c0, w):
        main = u_ref[:, c0:c0 + w].astype(F32)
        side = jnp.where(side_on, xh_ref[:, c0:c0 + w].astype(F32), 0.0)
        buf = jnp.concatenate([main, side] if rev else [side, main], axis=0)
        x3 = buf.reshape((ch + SSD_HALO) // 8, 8, w)
        first = 0 if rev else SSD_HALO // 8
        acc = cbias_ref[:, c0:c0 + w]
        for kk in range(SSM_CONV):
            s = (SSM_CONV - 1 - kk) * (1 if rev else -1)
            acc = acc + _shift_tiles(x3, s, first, ch // 8) * cw_ref[kk:kk + 1, c0:c0 + w]
        return _silu(acc).reshape(ch, w)

    dtv = _softplus(dt_ref[...] + dtb_ref[...])
    la = dtv * (-jnp.exp(alog_ref[...]))
    ri = lax.broadcasted_iota(jnp.int32, (ch, ch), 0)
    ci = lax.broadcasted_iota(jnp.int32, (ch, ch), 1)
    keep = (ci >= ri) if rev else (ci <= ri)
    acum = _dot01(jnp.where(keep, 1.0, 0.0).astype(BF16), la, left=True)
    last = 0 if rev else ch - 1
    total = acum[last:last + 1, :]
    wcol = jnp.exp(total - acum) * dtv
    offv = jnp.exp(acum)
    acum2 = acum * math.log2(math.e)
    acum2_t = acum2.T
    dt_t = dtv.T

    lane = lax.broadcasted_iota(jnp.int32, (ch, SSM_GW), 1)

    wexp = jnp.dot(wcol.astype(BF16), e_ref[...], preferred_element_type=F32)
    sexp = jnp.dot(offv.astype(BF16), e_ref[...], preferred_element_type=F32)
    cdec = _dot01(e_ref[...], jnp.broadcast_to(offv[last:last + 1, :], (8, DT_PAD)), left=False)[0:1, :]

    for g in range(SSM_GROUPS):
        gc = slice(g * SSM_GW, (g + 1) * SSM_GW)
        xg = load(g * SSM_GW, SSM_GW)
        xg16 = xg.astype(BF16)
        bg16 = load(D_INNER + g * SSM_STATE, SSM_STATE).astype(BF16)
        xw = (xg * wexp[:, gc]).astype(BF16)
        st_t = lax.dot_general(bg16, xw, (((0,), (0,)), ((), ())), preferred_element_type=F32)
        sg = sexp[:, gc]
        ht_old = ht_ref[g]
        if write_y:
            cg16 = load(D_INNER + SSM_BC + g * SSM_STATE, SSM_STATE).astype(BF16)
            cbm = lax.dot_general(cg16, bg16, (((1,), (1,)), ((), ())), preferred_element_type=F32)
            yoff = jnp.dot(cg16, ht_old.astype(BF16), preferred_element_type=F32)
            ms, xs = [], []
            for j in range(SSM_HPG):
                col = off + SSM_HPG * g + j
                seg = acum2[:, col:col + 1] - acum2_t[col:col + 1, :]
                lmat = jnp.exp2(jnp.where(keep, seg, -jnp.inf))
                ms.append((cbm * lmat * dt_t[col:col + 1, :]).astype(BF16))
                in_head = (lane >= j * SSM_HEAD_DIM) & (lane < (j + 1) * SSM_HEAD_DIM)
                xs.append(jnp.where(in_head, xg16, jnp.zeros_like(xg16)))
            ydiag = jnp.dot(jnp.concatenate(ms, axis=1), jnp.concatenate(xs, axis=0),
                            preferred_element_type=F32)
            y = ydiag + yoff * sg + dsk_ref[:, g * SSM_GW:(g + 1) * SSM_GW] * xg
            cols = slice(g * SSM_GW, (g + 1) * SSM_GW)
            if combine:
                y = y + yf_ref[:, cols].astype(F32)
            y_ref[:, cols] = y.astype(y_ref.dtype)
        ht_ref[g] = ht_old * cdec[:, gc] + st_t

    @pl.when(k == nc - 1)
    def _():
        hfin_ref[...] = ht_ref[...]


def _ssd(u, dt, h0, dt_bias, a_log, d_skip, taps, *, rev, write_y, combine=None, y_dtype=F32):
    nb, t, _ = u.shape
    nc = t // SSM_CHUNK
    hb = SSM_CHUNK // SSD_HALO
    if rev:
        cidx = lambda k: nc - 1 - k
        hidx = lambda k: jnp.minimum((nc - k) * hb, t // SSD_HALO - 1)
    else:
        cidx = lambda k: k
        hidx = lambda k: jnp.maximum(k * hb - 1, 0)
    chunk = lambda w: pl.BlockSpec((None, SSM_CHUNK, w), lambda b, k: (b, cidx(k), 0))
    state = pl.BlockSpec((None, SSM_GROUPS, SSM_STATE, SSM_GW), lambda b, k: (b, 0, 0, 0))
    in_specs = [chunk(SSM_CONV_CH), pl.BlockSpec((None, SSD_HALO, SSM_CONV_CH), lambda b, k: (b, hidx(k), 0)),
                _const_spec((SSM_CONV, SSM_CONV_CH)), _const_spec((1, SSM_CONV_CH))]
    args = [u, u, taps[0], taps[1]]
    scratch = [pltpu.VMEM((SSM_GROUPS, SSM_STATE, SSM_GW), F32)]
    off = SSM_HEADS if rev else 0
    spread = jnp.repeat(jnp.eye(DT_PAD, dtype=BF16)[:, off:off + SSM_HEADS], SSM_HEAD_DIM, axis=1)
    in_specs += [chunk(DT_PAD), _const_spec((1, DT_PAD)), _const_spec((1, DT_PAD)), _const_spec((1, D_INNER)),
                 _const_spec((DT_PAD, D_INNER)), state]
    args += [dt, dt_bias, a_log, d_skip, spread, h0]
    out_specs, out_shape = [], []
    if combine is not None:
        in_specs.append(chunk(D_INNER))
        args.append(combine)
    if write_y:
        out_specs.append(chunk(D_INNER))
        out_shape.append(jax.ShapeDtypeStruct((nb, t, D_INNER), y_dtype))
    out_specs.append(state)
    out_shape.append(jax.ShapeDtypeStruct((nb, SSM_GROUPS, SSM_STATE, SSM_GW), F32))
    outs = pl.pallas_call(
        functools.partial(_ssd_kernel, rev=rev, nc=nc, write_y=write_y, combine=combine is not None),
        grid=(nb, nc),
        in_specs=in_specs, out_specs=out_specs, out_shape=out_shape,
        scratch_shapes=scratch,
        compiler_params=_params(("parallel", "arbitrary")),
    )(*args)
    return outs if write_y else (None, outs[0])


def _resid_kernel(*refs, gate_row, ssm_gate):
    if ssm_gate:
        a_ref, z_ref, ng_ref, w_ref, b_ref, x_ref, mod_ref, o_ref = refs
        parts = []
        for g in range(SSM_GROUPS):
            cols = slice(g * SSM_GW, (g + 1) * SSM_GW)
            yg = a_ref[:, cols].astype(F32) * _silu(z_ref[:, cols].astype(F32))
            parts.append((_rms(yg) * ng_ref[:, cols]).astype(BF16))
        a = jnp.concatenate(parts, axis=1)
    else:
        a_ref, w_ref, b_ref, x_ref, mod_ref, o_ref = refs
        a = a_ref[...]
    y = jnp.dot(a, w_ref[...], preferred_element_type=F32) + b_ref[...]
    o_ref[...] = x_ref[...] + mod_ref[gate_row:gate_row + 1, :] * y


def _resid_matmul(a, w, bias, x, mod, layer, row, gate_row, ssm_gate=None):
    nb, t, kdim = a.shape
    tm = _tile(t, TOKEN_TILE)
    tok = lambda wd: pl.BlockSpec((None, tm, wd), lambda b, i: (b, i, 0))
    in_specs, args = [tok(kdim)], [a]
    if ssm_gate is not None:
        in_specs += [tok(kdim), _const_spec((1, kdim))]
        args += list(ssm_gate)
    in_specs += [_const_spec((kdim, D)), _const_spec((1, D)), tok(D), _mod_spec(layer, row)]
    args += [w, bias, x, mod]
    return pl.pallas_call(
        functools.partial(_resid_kernel, gate_row=gate_row, ssm_gate=ssm_gate is not None),
        grid=(nb, t // tm),
        in_specs=in_specs,
        out_specs=tok(D),
        out_shape=jax.ShapeDtypeStruct((nb, t, D), F32),
        compiler_params=_params(("parallel", "parallel")),
    )(*args)


def _mamba_layer(x, xc, mod, layer, nb, g1, p, need_ctx):
    wz, wx, wd, cw, cbias, dtb, alog, dsk, ng, wout = p
    z_l, xbc_l, dt_l = _mamba_in(x, mod, layer, None, g1, wz, wx, wd)
    z_c, xbc_c, dt_c = _mamba_in(xc, mod, layer, nb, g1, wz, wx, wd)
    h0 = jnp.zeros((x.shape[0], SSM_GROUPS, SSM_STATE, SSM_GW), F32)
    dirp = lambda d: (dtb, alog, dsk[d])
    taps = lambda d: (cw[d], cbias[d])
    yf_c, hf = _ssd(xbc_c, dt_c, h0, *dirp(0), rev=False, write_y=need_ctx, taps=taps(0))
    yf_l, _ = _ssd(xbc_l, dt_l, hf, *dirp(0), rev=False, write_y=True, taps=taps(0))
    y_c, hb = _ssd(xbc_c, dt_c, h0, *dirp(1), rev=True, write_y=need_ctx, taps=taps(1),
                   combine=yf_c if need_ctx else None, y_dtype=BF16)
    y_l, _ = _ssd(xbc_l, dt_l, hb, *dirp(1), rev=True, write_y=True, taps=taps(1), combine=yf_l, y_dtype=BF16)
    zero_b = jnp.zeros((1, D), F32)
    x = _resid_matmul(y_l, wout, zero_b, x, mod, layer, None, 2, ssm_gate=(z_l, ng))
    if need_ctx:
        xc = _resid_matmul(y_c, wout, zero_b, xc, mod, layer, nb, 2, ssm_gate=(z_c, ng))
    return x, xc


def _qkv_kernel(*refs, rope):
    if rope:
        x_ref, mod_ref, g_ref, wq_ref, wk_ref, wv_ref, cos_ref, sin_ref, q_ref, k_ref, v_ref = refs
    else:
        x_ref, mod_ref, g_ref, wq_ref, wk_ref, wv_ref, q_ref, k_ref, v_ref = refs
    gs = g_ref[...] * (1.0 + mod_ref[1:2, :])
    h = (_rms(x_ref[...]) * gs + mod_ref[0:1, :]).astype(BF16)
    tm = h.shape[0]
    if rope:
        cos, sin = cos_ref[...], sin_ref[...]
        lane = lax.broadcasted_iota(jnp.int32, (tm, 128), 1)
        first = (lane % 32) < 16

    def rot(a_ref, w_ref, scale):
        for j in range(D // 128):
            blk = jnp.dot(h, w_ref[:, j * 128:(j + 1) * 128], preferred_element_type=F32)
            if scale != 1.0:
                blk = blk * scale
            if rope:
                partner = jnp.where(first, pltpu.roll(blk, 112, 1), pltpu.roll(blk, 16, 1))
                blk = blk * cos + partner * sin
            a_ref[:, j * 128:(j + 1) * 128] = blk.astype(a_ref.dtype)

    rot(q_ref, wq_ref, DA_HEAD_DIM ** -0.5 * math.log2(math.e))
    rot(k_ref, wk_ref, 1.0)
    v_ref[...] = jnp.dot(h, wv_ref[...], preferred_element_type=F32).astype(v_ref.dtype)


def _rope_tables(s):
    lane = jnp.arange(128)
    pos = jnp.arange(s)
    coord = jnp.where(((lane % 64) // 32 == 0)[None, :], (pos // GRID_W)[:, None], (pos % GRID_W)[:, None])
    n_freq = DA_HEAD_DIM // 4
    inv = ROPE_THETA ** (-(lane % n_freq).astype(F32) / n_freq)
    ang = coord.astype(F32) * inv[None, :]
    sign = jnp.where((lane % 32) < 16, -1.0, 1.0)
    return jnp.cos(ang), jnp.sin(ang) * sign[None, :]


def _qkv(x, mod, layer, row, g, wq, wk, wv, tables):
    nb, t, _ = x.shape
    tm = _tile(t, TOKEN_TILE)
    tok = lambda: pl.BlockSpec((None, tm, D), lambda b, i: (b, i, 0))
    in_specs = [tok(), _mod_spec(layer, row), _const_spec((1, D))] + [_const_spec((D, D))] * 3
    args = [x, mod, g, wq, wk, wv]
    if tables is not None:
        in_specs += [pl.BlockSpec((tm, 128), lambda b, i: (i, 0))] * 2
        args += list(tables)
    return pl.pallas_call(
        functools.partial(_qkv_kernel, rope=tables is not None),
        grid=(nb, t // tm),
        in_specs=in_specs, out_specs=[tok()] * 3,
        out_shape=[jax.ShapeDtypeStruct((nb, t, D), BF16)] * 3,
        compiler_params=_params(("parallel", "parallel")),
    )(*args)


def _attn_kernel(*refs, nseg, lam_init):
    lam_ref, ng_ref, q_ref = refs[:3]
    kv = refs[3:3 + 2 * nseg]
    o_ref = refs[3 + 2 * nseg]
    q = q_ref[...]
    tq = q.shape[0]
    lane = lax.broadcasted_iota(jnp.int32, q.shape, 1)
    zero = jnp.zeros_like(q)
    qs = [jnp.where((lane >= e * DA_HEAD_DIM) & (lane < (e + 1) * DA_HEAD_DIM), q, zero) for e in range(2)]
    scores = [[lax.dot_general(qe, kv[2 * i][...], (((1,), (1,)), ((), ())), preferred_element_type=F32)
               for i in range(nseg)] for qe in qs]
    vs = [jnp.concatenate([kv[2 * i + 1][...], jnp.ones(kv[2 * i + 1].shape, BF16)], axis=1) for i in range(nseg)]
    halves = []
    for e in range(2):
        m = scores[e][0].max(axis=-1, keepdims=True)
        for s in scores[e][1:]:
            m = jnp.maximum(m, s.max(axis=-1, keepdims=True))
        acc = jnp.zeros((tq, 2 * DA_V), F32)
        for i in range(nseg):
            p = jnp.exp2(scores[e][i] - m)
            acc = acc + jnp.dot(p.astype(BF16), vs[i], preferred_element_type=F32)
        halves.append(acc[:, :DA_V] / acc[:, DA_V:DA_V + 1])
    lp = lam_ref[...]
    lam = (jnp.exp(jnp.sum(lp[0:1] * lp[1:2], axis=-1, keepdims=True))
           - jnp.exp(jnp.sum(lp[2:3] * lp[3:4], axis=-1, keepdims=True)) + lam_init)
    o = halves[0] - lam * halves[1]
    o_ref[...] = (_rms(o) * ng_ref[...] * (1.0 - lam_init)).astype(o_ref.dtype)


def _attention(q, segs, lam_p, ng, lam_init):
    nb, t, _ = q.shape
    tq = _tile(t, ATTN_Q_TILE)
    in_specs = [_const_spec((4, DA_HEAD_DIM)), _const_spec((1, DA_V)),
                pl.BlockSpec((None, tq, DA_V), lambda b, h, i: (b, i, h))]
    args = [lam_p, ng, q]
    for k, v in segs:
        tk = k.shape[1]
        in_specs += [pl.BlockSpec((None, tk, DA_V), lambda b, h, i: (b, 0, h))] * 2
        args += [k, v]
    return pl.pallas_call(
        functools.partial(_attn_kernel, nseg=len(segs), lam_init=lam_init),
        grid=(nb, DA_HEADS, t // tq),
        in_specs=in_specs,
        out_specs=pl.BlockSpec((None, tq, DA_V), lambda b, h, i: (b, i, h)),
        out_shape=jax.ShapeDtypeStruct((nb, t, D), BF16),
        compiler_params=_params(("parallel", "parallel", "arbitrary")),
    )(*args)


def _attn_layer(x, xc, mod, layer, nb, g1, p, need_ctx):
    wq, wk, wv, lam_p, ng, wout = p
    lam_init = 0.8 - 0.6 * math.exp(-0.3 * layer)
    q_l, k_l, v_l = _qkv(x, mod, layer, None, g1, wq, wk, wv, _rope_tables(x.shape[1]))
    q_c, k_c, v_c = _qkv(xc, mod, layer, nb, g1, wq, wk, wv, None)
    zero_b = jnp.zeros((1, D), F32)
    o_l = _attention(q_l, [(k_c, v_c), (k_l, v_l)], lam_p, ng, lam_init)
    x = _resid_matmul(o_l, wout, zero_b, x, mod, layer, None, 2)
    if need_ctx:
        o_c = _attention(q_c, [(k_c, v_c)], lam_p, ng, lam_init)
        xc = _resid_matmul(o_c, wout, zero_b, xc, mod, layer, nb, 2)
    return x, xc


def _conf_kernel(xm_ref, xp_ref, xn_ref, mod_ref, g_ref, w1_ref, b1_ref, dw_ref, db_ref, lg_ref, lb_ref,
                 w2_ref, b2_ref, o_ref, h_ref, u_ref, c_ref, sh_ref, *, tm, nt):
    t = pl.program_id(1)
    hl = CONF_HALO
    gs = g_ref[...] * (1.0 + mod_ref[1:2, :])
    shift = mod_ref[0:1, :]
    mod = lambda x: _rms(x) * gs + shift
    h_ref[...] = jnp.concatenate([mod(xp_ref[...]), mod(xm_ref[...]), mod(xn_ref[...])], axis=0).astype(BF16)
    rowid = lax.broadcasted_iota(jnp.int32, (tm + 2 * hl, 1), 0)
    valid = ((rowid >= hl) | (t > 0)) & ((rowid < hl + tm) | (t < nt - 1))

    rb, cw = 64, CONF_COLS
    pad = CONF_KERNEL // 2
    nt8 = (tm + 2 * hl) // 8
    for c0 in range(0, D, cw):
        a = jnp.dot(h_ref[...], w1_ref[c0 // cw], preferred_element_type=F32) + b1_ref[c0 // cw]
        u = jnp.where(valid, a[:, :cw] * _sigmoid(a[:, cw:]), 0.0)
        u_ref[:, c0:c0 + cw] = u
        u3 = u.reshape(nt8, 8, cw)
        for r in range(1, 8):
            sh_ref[r - 1, 8:, c0:c0 + cw] = _shift_tiles(u3, -r, 1, nt8 - 1).reshape(8 * (nt8 - 1), cw)
        for r0 in range(0, tm, rb):
            acc = jnp.broadcast_to(db_ref[:, c0:c0 + cw], (rb, cw))
            for kk in range(CONF_KERNEL):
                d = kk - pad
                r = (-d) % 8
                s = hl + d + r + r0
                src = u_ref if r == 0 else sh_ref.at[r - 1]
                acc = acc + src[s:s + rb, c0:c0 + cw] * dw_ref[kk:kk + 1, c0:c0 + cw]
            c_ref[r0:r0 + rb, c0:c0 + cw] = acc

    cv = c_ref[...]
    mu = jnp.mean(cv, axis=-1, keepdims=True)
    xc = cv - mu
    ln = xc * lax.rsqrt(jnp.mean(xc * xc, axis=-1, keepdims=True) + EPS) * lg_ref[...] + lb_ref[...]
    y = jnp.dot(_silu(ln).astype(BF16), w2_ref[...], preferred_element_type=F32) + b2_ref[...]
    o_ref[...] = xm_ref[...] + mod_ref[2:3, :] * y


def _conformer(x, mod, layer, row, g, p):
    w1, b1, dw, db, lg, lb, w2, b2 = p
    nb, t, _ = x.shape
    tm = _tile(t, 256)
    nt = t // tm
    hl = CONF_HALO
    r = tm // hl
    return pl.pallas_call(
        functools.partial(_conf_kernel, tm=tm, nt=nt),
        grid=(nb, nt),
        in_specs=[pl.BlockSpec((None, tm, D), lambda b, i: (b, i, 0)),
                  pl.BlockSpec((None, hl, D), lambda b, i: (b, jnp.maximum(i * r - 1, 0), 0)),
                  pl.BlockSpec((None, hl, D), lambda b, i: (b, jnp.minimum((i + 1) * r, t // hl - 1), 0)),
                  _mod_spec(layer, row), _const_spec((1, D)),
                  _const_spec((D // CONF_COLS, D, 2 * CONF_COLS)), _const_spec((D // CONF_COLS, 1, 2 * CONF_COLS)),
                  _const_spec((CONF_KERNEL, D)), _const_spec((1, D)), _const_spec((1, D)), _const_spec((1, D)),
                  _const_spec((D, D)), _const_spec((1, D))],
        out_specs=pl.BlockSpec((None, tm, D), lambda b, i: (b, i, 0)),
        out_shape=jax.ShapeDtypeStruct((nb, t, D), F32),
        scratch_shapes=[pltpu.VMEM((tm + 2 * hl, D), BF16), pltpu.VMEM((tm + 2 * hl, D), F32),
                        pltpu.VMEM((tm, D), F32), pltpu.VMEM((7, tm + 2 * hl, D), F32)],
        compiler_params=_params(("parallel", "parallel")),
    )(x, x, x, mod, g, w1, b1, dw, db, lg, lb, w2, b2)


def _ffn_kernel(*refs, tm, nt, final):
    if final:
        (xm_ref, xp_ref, xn_ref, mod_ref, g_ref, wu_ref, cw_ref, cb_ref, wd_ref, fg_ref,
         o_ref, h_ref, act_ref) = refs
    else:
        (xm_ref, xp_ref, xn_ref, mod_ref, g_ref, wu_ref, cw_ref, cb_ref, wd_ref,
         o_ref, h_ref, act_ref) = refs
    t = pl.program_id(1)
    hl = FFN_HALO
    gs = g_ref[...] * (1.0 + mod_ref[4:5, :])
    shift = mod_ref[3:4, :]
    mod = lambda x: _rms(x) * gs + shift
    hp = jnp.where(t > 0, mod(xp_ref[...]), 0.0)
    hn = jnp.where(t < nt - 1, mod(xn_ref[...]), 0.0)
    h_ref[...] = jnp.concatenate([hp, mod(xm_ref[...]), hn], axis=0).astype(BF16)

    nt8, nm8 = (tm + 2 * hl) // 8, tm // 8
    y, k0 = None, 0
    up = lambda j: jnp.dot(h_ref[...], wu_ref[j], preferred_element_type=F32).reshape(nt8, 8, 2 * FFN_CHUNK)
    u_next = up(0)
    for j in range(FFN_NCHUNK):
        u3 = u_next
        if j + 1 < FFN_NCHUNK:
            u_next = up(j + 1)
        w = cw_ref[j]
        cv = cb_ref[j]
        for kk in range(3):
            cv = cv + _shift_tiles(u3, kk - 1, 1, nm8) * w[kk:kk + 1, :]
        cv = cv.reshape(tm, 2 * FFN_CHUNK)
        act_ref[:, j * FFN_CHUNK:(j + 1) * FFN_CHUNK] = (_silu(cv[:, :FFN_CHUNK]) * cv[:, FFN_CHUNK:]).astype(BF16)
        if (j + 1) % FFN_DOWN_EVERY == 0 or j == FFN_NCHUNK - 1:
            k1 = (j + 1) * FFN_CHUNK
            part = jnp.dot(act_ref[:, k0:k1], wd_ref[k0:k1, :], preferred_element_type=F32)
            y = part if y is None else y + part
            k0 = k1
    out = xm_ref[...] + mod_ref[5:6, :] * y
    if final:
        out = _rms(out) * fg_ref[...]
    o_ref[...] = out


def _ffn(x, mod, layer, row, g, p, final_g=None):
    wu, cw, cb, wd = p
    nb, t, _ = x.shape
    tm = _tile(t, TOKEN_TILE)
    nt = t // tm
    hl = FFN_HALO
    r = tm // hl
    final = final_g is not None
    in_specs = [pl.BlockSpec((None, tm, D), lambda b, i: (b, i, 0)),
                pl.BlockSpec((None, hl, D), lambda b, i: (b, jnp.maximum(i * r - 1, 0), 0)),
                pl.BlockSpec((None, hl, D), lambda b, i: (b, jnp.minimum((i + 1) * r, t // hl - 1), 0)),
                _mod_spec(layer, row), _const_spec((1, D)),
                _const_spec((FFN_NCHUNK, D, 2 * FFN_CHUNK)), _const_spec((FFN_NCHUNK, 3, 2 * FFN_CHUNK)),
                _const_spec((FFN_NCHUNK, 1, 2 * FFN_CHUNK)), _const_spec((FFN_HIDDEN, D))]
    args = [x, x, x, mod, g, wu, cw, cb, wd]
    if final:
        in_specs.append(_const_spec((1, D)))
        args.append(final_g)
    return pl.pallas_call(
        functools.partial(_ffn_kernel, tm=tm, nt=nt, final=final),
        grid=(nb, nt),
        in_specs=in_specs,
        out_specs=pl.BlockSpec((None, tm, D), lambda b, i: (b, i, 0)),
        out_shape=jax.ShapeDtypeStruct((nb, t, D), F32),
        scratch_shapes=[pltpu.VMEM((tm + 2 * hl, D), BF16), pltpu.VMEM((tm, FFN_HIDDEN), BF16)],
        compiler_params=_params(("parallel", "parallel")),
    )(*args)


def _pair_cols(a, nchunk, width):
    lead = a.shape[:-1]
    a = a.reshape(lead + (2, nchunk, width))
    a = jnp.moveaxis(a, -2, 0)
    return a.reshape((nchunk,) + lead + (2 * width,))


def _ffn_weights(w_up, conv_w, conv_b, w_down):
    pair = lambda a: _pair_cols(a, FFN_NCHUNK, FFN_CHUNK)
    return (pair(w_up).astype(BF16), pair(conv_w), pair(conv_b[None, :]),
            w_down.astype(BF16))


def kernel(x, c, ctx, c_ctx, mod_w, mod_b, norm1_g, norm2_g, ffn_w_up, ffn_conv_w, ffn_conv_b, ffn_w_down,
           ssm_w_in, ssm_conv_w, ssm_conv_b, ssm_dt_bias, ssm_a_log, ssm_d, ssm_norm_g, ssm_w_out, attn_w_in,
           attn_lambda, attn_norm_g, attn_w_out, conf_w_pw1, conf_b_pw1, conf_dw_w, conf_dw_b, conf_ln_g,
           conf_ln_b, conf_w_pw2, conf_b_pw2, final_g):
    depth = mod_w.shape[0]
    nb = x.shape[0]
    mod = _mod_table(c, c_ctx, mod_w, mod_b)
    xc = ctx
    row = lambda a: a.reshape(1, -1)
    pad_dt = lambda a: jnp.pad(a.reshape(1, -1), ((0, 0), (0, DT_PAD - 2 * SSM_HEADS)))
    for i in range(depth):
        kind, j = i % N_MIXERS, i // N_MIXERS
        need_ctx = i < depth - 1
        g1 = row(norm1_g[i])
        if kind == 0:
            w_in = ssm_w_in[j]
            o_x, o_dt = D_INNER, D_INNER + SSM_CONV_CH
            wd = jnp.pad(w_in[:, o_dt:], ((0, 0), (0, DT_PAD - 2 * SSM_HEADS)))
            p = (w_in[:, :o_x].astype(BF16), w_in[:, o_x:o_dt].astype(BF16), wd.astype(BF16),
                 ssm_conv_w[j], ssm_conv_b[j][:, None, :], pad_dt(ssm_dt_bias[j]), pad_dt(ssm_a_log[j]),
                 jnp.repeat(ssm_d[j], SSM_HEAD_DIM, axis=-1)[:, None, :], row(ssm_norm_g[j]),
                 ssm_w_out[j].astype(BF16))
            x, xc = _mamba_layer(x, xc, mod, i, nb, g1, p, need_ctx)
        elif kind == 1:
            w_in = attn_w_in[j].astype(BF16)
            p = (w_in[:, :D], w_in[:, D:2 * D], w_in[:, 2 * D:], attn_lambda[j], row(attn_norm_g[j]),
                 attn_w_out[j].astype(BF16))
            x, xc = _attn_layer(x, xc, mod, i, nb, g1, p, need_ctx)
        else:
            p = (_pair_cols(conf_w_pw1[j], D // CONF_COLS, CONF_COLS).astype(BF16),
                 _pair_cols(row(conf_b_pw1[j]), D // CONF_COLS, CONF_COLS), conf_dw_w[j], row(conf_dw_b[j]),
                 row(conf_ln_g[j]), row(conf_ln_b[j]), conf_w_pw2[j].astype(BF16), row(conf_b_pw2[j]))
            x = _conformer(x, mod, i, None, g1, p)
            if need_ctx:
                xc = _conformer(xc, mod, i, nb, g1, p)
        fp = _ffn_weights(ffn_w_up[i], ffn_conv_w[i], ffn_conv_b[i], ffn_w_down[i])
        g2 = row(norm2_g[i])
        x = _ffn(x, mod, i, None, g2, fp, final_g=row(final_g) if i == depth - 1 else None)
        if need_ctx:
            xc = _ffn(xc, mod, i, nb, g2, fp)
    return x
```

```python
import functools
import math

import jax
import jax.numpy as jnp
from jax import lax
from jax.experimental import pallas as pl
from jax.experimental.pallas import tpu as pltpu

F32 = jnp.float32
BF16 = jnp.bfloat16
HIGHEST = lax.Precision.HIGHEST

D = 1024
N_MOD = 6
N_MIXERS = 3
EPS = 1e-6
GRID_W = 64
ROPE_THETA = 10000.0

SSM_HEAD_DIM = 64
SSM_HEADS = 32
SSM_GROUPS = 8
SSM_HPG = SSM_HEADS // SSM_GROUPS
SSM_STATE = 128
SSM_CONV = 4
SSM_CHUNK = 128
D_INNER = SSM_HEADS * SSM_HEAD_DIM
SSM_GW = SSM_HPG * SSM_HEAD_DIM
SSM_BC = SSM_GROUPS * SSM_STATE
SSM_CONV_CH = D_INNER + 2 * SSM_BC
DT_PAD = 128
SSD_HALO = 16

DA_HEAD_DIM = 64
DA_HEADS = D // (2 * DA_HEAD_DIM)
DA_V = 2 * DA_HEAD_DIM

CONF_KERNEL = 31
CONF_HALO = 16
CONF_COLS = 256

FFN_HIDDEN = 2816
FFN_CHUNK = 256
FFN_NCHUNK = FFN_HIDDEN // FFN_CHUNK
FFN_HALO = 8
FFN_DOWN_EVERY = 4

VMEM_LIMIT = 56 * 1024 * 1024
TOKEN_TILE = 512
ATTN_Q_TILE = 512
ATTN_HEADS_PER_STEP = 2
QKV_COLS = 512


def _sigmoid(x):
    return 1.0 / (1.0 + jnp.exp2(x * -math.log2(math.e)))


def _silu(x):
    return x * _sigmoid(x)


def _softplus(x):
    return jnp.maximum(x, 0.0) + jnp.log(1.0 + jnp.exp(-jnp.abs(x)))


def _rms(x):
    return x * lax.rsqrt(jnp.mean(x * x, axis=-1, keepdims=True) + EPS)


def _shift_tiles(x3, s, first, count):
    if s == 0:
        return x3[first:first + count]
    sub = lax.broadcasted_iota(jnp.int32, (count,) + x3.shape[1:], 1)
    rk = pltpu.roll(x3, (-s) % 8, 1)
    if s < 0:
        return jnp.where(sub >= -s, rk[first:first + count], rk[first - 1:first - 1 + count])
    return jnp.where(sub < 8 - s, rk[first:first + count], rk[first + 1:first + 1 + count])


def _dot01(m01, x, *, left):
    out, rest = None, x
    for _ in range(3):
        piece = rest.astype(BF16)
        part = jnp.dot(m01, piece, preferred_element_type=F32) if left else \
            jnp.dot(piece, m01, preferred_element_type=F32)
        out = part if out is None else out + part
        rest = rest - piece.astype(F32)
    return out


def _params(sem, vmem=VMEM_LIMIT):
    return pltpu.CompilerParams(dimension_semantics=sem, vmem_limit_bytes=vmem)


def _mod_spec(layer, row):
    if row is None:
        return pl.BlockSpec((None, None, N_MOD, D), lambda b, *_: (layer, b, 0, 0))
    return pl.BlockSpec((None, None, N_MOD, D), lambda b, *_: (layer, row, 0, 0))


def _const_spec(shape):
    nd = len(shape)
    return pl.BlockSpec(shape, lambda *_: (0,) * nd)


def _tile(n, pref):
    return pref if n % pref == 0 else n


def _mod_kernel(s_ref, w_ref, b_ref, o_ref):
    s = _silu(s_ref[...])
    o_ref[...] = jnp.dot(s, w_ref[...], preferred_element_type=F32, precision=HIGHEST) + b_ref[...]


def _mod_table(c, c_ctx, mod_w, mod_b):
    depth = mod_w.shape[0]
    nb = c.shape[0]
    rows = -(-(nb + 1) // 8) * 8
    s = jnp.zeros((rows, D), F32).at[:nb].set(c).at[nb].set(c_ctx)
    tn = 1536
    out = pl.pallas_call(
        _mod_kernel,
        grid=(depth, N_MOD * D // tn),
        in_specs=[pl.BlockSpec((rows, D), lambda i, n: (0, 0)),
                  pl.BlockSpec((None, D, tn), lambda i, n: (i, 0, n)),
                  pl.BlockSpec((None, 1, tn), lambda i, n: (i, 0, n))],
        out_specs=pl.BlockSpec((None, rows, tn), lambda i, n: (i, 0, n)),
        out_shape=jax.ShapeDtypeStruct((depth, rows, N_MOD * D), F32),
        compiler_params=_params(("arbitrary", "arbitrary")),
    )(s, mod_w, mod_b.reshape(depth, 1, N_MOD * D))
    return out.reshape(depth, rows, N_MOD, D)


def _mamba_in_kernel(x_ref, mod_ref, g_ref, wz_ref, wx_ref, wd_ref, z_ref, xbc_ref, dt_ref):
    gs = g_ref[...] * (1.0 + mod_ref[1:2, :])
    h = (_rms(x_ref[...]) * gs + mod_ref[0:1, :]).astype(BF16)
    for n in range(0, D_INNER, 1024):
        z_ref[:, n:n + 1024] = jnp.dot(h, wz_ref[:, n:n + 1024], preferred_element_type=F32).astype(z_ref.dtype)
    for n in range(0, SSM_CONV_CH, 1024):
        xbc_ref[:, n:n + 1024] = jnp.dot(h, wx_ref[:, n:n + 1024],
                                         preferred_element_type=F32).astype(xbc_ref.dtype)
    dt_ref[...] = jnp.dot(h, wd_ref[...], preferred_element_type=F32)


def _mamba_in(x, mod, layer, row, g, wz, wx, wd):
    nb, t, _ = x.shape
    tm = _tile(t, TOKEN_TILE)
    tok = lambda w: pl.BlockSpec((None, tm, w), lambda b, i: (b, i, 0))
    return pl.pallas_call(
        _mamba_in_kernel,
        grid=(nb, t // tm),
        in_specs=[tok(D), _mod_spec(layer, row), _const_spec((1, D)),
                  _const_spec((D, D_INNER)), _const_spec((D, SSM_CONV_CH)), _const_spec((D, DT_PAD))],
        out_specs=[tok(D_INNER), tok(SSM_CONV_CH), tok(DT_PAD)],
        out_shape=[jax.ShapeDtypeStruct((nb, t, D_INNER), BF16),
                   jax.ShapeDtypeStruct((nb, t, SSM_CONV_CH), BF16),
                   jax.ShapeDtypeStruct((nb, t, DT_PAD), F32)],
        compiler_params=_params(("parallel", "parallel")),
    )(x, mod, g, wz, wx, wd)


def _ssd_kernel(*refs, rev, nc, write_y, combine):
    it = iter(refs)
    u_ref, xh_ref, cw_ref, cbias_ref = (next(it) for _ in range(4))
    dt_ref, dtb_ref, alog_ref, dsk_ref, e_ref, h0_ref = (next(it) for _ in range(6))
    if combine:
        yf_ref = next(it)
    y_ref = next(it) if write_y else None
    hfin_ref = next(it)
    ht_ref = next(it)

    k = pl.program_id(1)
    ch = SSM_CHUNK
    off = SSM_HEADS if rev else 0

    @pl.when(k == 0)
    def _():
        ht_ref[...] = h0_ref[...]

    c = (nc - 1 - k) if rev else k
    side_on = (c < nc - 1) if rev else (c > 0)

    def load(c0, w):
        main = u_ref[:, c0:c0 + w].astype(F32)
        side = jnp.where(side_on, xh_ref[:, c0:c0 + w].astype(F32), 0.0)
        buf = jnp.concatenate([main, side] if rev else [side, main], axis=0)
        x3 = buf.reshape((ch + SSD_HALO) // 8, 8, w)
        first = 0 if rev else SSD_HALO // 8
        acc = cbias_ref[:, c0:c0 + w]
        for kk in range(SSM_CONV):
            s = (SSM_CONV - 1 - kk) * (1 if rev else -1)
            acc = acc + _shift_tiles(x3, s, first, ch // 8) * cw_ref[kk:kk + 1, c0:c0 + w]
        return _silu(acc).reshape(ch, w)

    dtv = _softplus(dt_ref[...] + dtb_ref[...])
    la = dtv * (-jnp.exp(alog_ref[...]))
    ri = lax.broadcasted_iota(jnp.int32, (ch, ch), 0)
    ci = lax.broadcasted_iota(jnp.int32, (ch, ch), 1)
    keep = (ci >= ri) if rev else (ci <= ri)
    acum = _dot01(jnp.where(keep, 1.0, 0.0).astype(BF16), la, left=True)
    last = 0 if rev else ch - 1
    total = acum[last:last + 1, :]
    wcol = jnp.exp(total - acum) * dtv
    offv = jnp.exp(acum)
    acum2 = acum * math.log2(math.e)
    acum2_t = acum2.T
    dt_t = dtv.T

    lane = lax.broadcasted_iota(jnp.int32, (ch, SSM_GW), 1)

    wexp = jnp.dot(wcol.astype(BF16), e_ref[...], preferred_element_type=F32)
    sexp = jnp.dot(offv.astype(BF16), e_ref[...], preferred_element_type=F32)
    cdec = _dot01(e_ref[...], jnp.broadcast_to(offv[last:last + 1, :], (8, DT_PAD)), left=False)[0:1, :]

    for g in range(SSM_GROUPS):
        gc = slice(g * SSM_GW, (g + 1) * SSM_GW)
        xg = load(g * SSM_GW, SSM_GW)
        xg16 = xg.astype(BF16)
        bg16 = load(D_INNER + g * SSM_STATE, SSM_STATE).astype(BF16)
        xw = (xg * wexp[:, gc]).astype(BF16)
        st_t = lax.dot_general(bg16, xw, (((0,), (0,)), ((), ())), preferred_element_type=F32)
        sg = sexp[:, gc]
        ht_old = ht_ref[g]
        if write_y:
            cg16 = load(D_INNER + SSM_BC + g * SSM_STATE, SSM_STATE).astype(BF16)
            cbm = lax.dot_general(cg16, bg16, (((1,), (1,)), ((), ())), preferred_element_type=F32)
            yoff = jnp.dot(cg16, ht_old.astype(BF16), preferred_element_type=F32)
            ms, xs = [], []
            for j in range(SSM_HPG):
                col = off + SSM_HPG * g + j
                seg = acum2[:, col:col + 1] - acum2_t[col:col + 1, :]
                lmat = jnp.exp2(jnp.where(keep, seg, -jnp.inf))
                ms.append((cbm * lmat * dt_t[col:col + 1, :]).astype(BF16))
                in_head = (lane >= j * SSM_HEAD_DIM) & (lane < (j + 1) * SSM_HEAD_DIM)
                xs.append(jnp.where(in_head, xg16, jnp.zeros_like(xg16)))
            ydiag = jnp.dot(jnp.concatenate(ms, axis=1), jnp.concatenate(xs, axis=0),
                            preferred_element_type=F32)
            y = ydiag + yoff * sg + dsk_ref[:, g * SSM_GW:(g + 1) * SSM_GW] * xg
            cols = slice(g * SSM_GW, (g + 1) * SSM_GW)
            if combine:
                y = y + yf_ref[:, cols].astype(F32)
            y_ref[:, cols] = y.astype(y_ref.dtype)
        ht_ref[g] = ht_old * cdec[:, gc] + st_t

    @pl.when(k == nc - 1)
    def _():
        hfin_ref[...] = ht_ref[...]


def _ssd(u, dt, h0, dt_bias, a_log, d_skip, taps, *, rev, write_y, combine=None, y_dtype=F32):
    nb, t, _ = u.shape
    nc = t // SSM_CHUNK
    hb = SSM_CHUNK // SSD_HALO
    if rev:
        cidx = lambda k: nc - 1 - k
        hidx = lambda k: jnp.minimum((nc - k) * hb, t // SSD_HALO - 1)
    else:
        cidx = lambda k: k
        hidx = lambda k: jnp.maximum(k * hb - 1, 0)
    chunk = lambda w: pl.BlockSpec((None, SSM_CHUNK, w), lambda b, k: (b, cidx(k), 0))
    state = pl.BlockSpec((None, SSM_GROUPS, SSM_STATE, SSM_GW), lambda b, k: (b, 0, 0, 0))
    in_specs = [chunk(SSM_CONV_CH), pl.BlockSpec((None, SSD_HALO, SSM_CONV_CH), lambda b, k: (b, hidx(k), 0)),
                _const_spec((SSM_CONV, SSM_CONV_CH)), _const_spec((1, SSM_CONV_CH))]
    args = [u, u, taps[0], taps[1]]
    scratch = [pltpu.VMEM((SSM_GROUPS, SSM_STATE, SSM_GW), F32)]
    off = SSM_HEADS if rev else 0
    spread = jnp.repeat(jnp.eye(DT_PAD, dtype=BF16)[:, off:off + SSM_HEADS], SSM_HEAD_DIM, axis=1)
    in_specs += [chunk(DT_PAD), _const_spec((1, DT_PAD)), _const_spec((1, DT_PAD)), _const_spec((1, D_INNER)),
                 _const_spec((DT_PAD, D_INNER)), state]
    args += [dt, dt_bias, a_log, d_skip, spread, h0]
    out_specs, out_shape = [], []
    if combine is not None:
        in_specs.append(chunk(D_INNER))
        args.append(combine)
    if write_y:
        out_specs.append(chunk(D_INNER))
        out_shape.append(jax.ShapeDtypeStruct((nb, t, D_INNER), y_dtype))
    out_specs.append(state)
    out_shape.append(jax.ShapeDtypeStruct((nb, SSM_GROUPS, SSM_STATE, SSM_GW), F32))
    outs = pl.pallas_call(
        functools.partial(_ssd_kernel, rev=rev, nc=nc, write_y=write_y, combine=combine is not None),
        grid=(nb, nc),
        in_specs=in_specs, out_specs=out_specs, out_shape=out_shape,
        scratch_shapes=scratch,
        compiler_params=_params(("parallel", "arbitrary")),
    )(*args)
    return outs if write_y else (None, outs[0])


def _resid_kernel(*refs, gate_row, ssm_gate):
    if ssm_gate:
        a_ref, z_ref, ng_ref, w_ref, b_ref, x_ref, mod_ref, o_ref = refs
        parts = []
        for g in range(SSM_GROUPS):
            cols = slice(g * SSM_GW, (g + 1) * SSM_GW)
            yg = a_ref[:, cols].astype(F32) * _silu(z_ref[:, cols].astype(F32))
            parts.append((_rms(yg) * ng_ref[:, cols]).astype(BF16))
        a = jnp.concatenate(parts, axis=1)
    else:
        a_ref, w_ref, b_ref, x_ref, mod_ref, o_ref = refs
        a = a_ref[...]
    y = jnp.dot(a, w_ref[...], preferred_element_type=F32) + b_ref[...]
    o_ref[...] = x_ref[...] + mod_ref[gate_row:gate_row + 1, :] * y


def _resid_matmul(a, w, bias, x, mod, layer, row, gate_row, ssm_gate=None):
    nb, t, kdim = a.shape
    tm = _tile(t, TOKEN_TILE)
    tok = lambda wd: pl.BlockSpec((None, tm, wd), lambda b, i: (b, i, 0))
    in_specs, args = [tok(kdim)], [a]
    if ssm_gate is not None:
        in_specs += [tok(kdim), _const_spec((1, kdim))]
        args += list(ssm_gate)
    in_specs += [_const_spec((kdim, D)), _const_spec((1, D)), tok(D), _mod_spec(layer, row)]
    args += [w, bias, x, mod]
    return pl.pallas_call(
        functools.partial(_resid_kernel, gate_row=gate_row, ssm_gate=ssm_gate is not None),
        grid=(nb, t // tm),
        in_specs=in_specs,
        out_specs=tok(D),
        out_shape=jax.ShapeDtypeStruct((nb, t, D), F32),
        compiler_params=_params(("parallel", "parallel")),
    )(*args)


def _mamba_layer(x, xc, mod, layer, nb, g1, p, need_ctx):
    wz, wx, wd, cw, cbias, dtb, alog, dsk, ng, wout = p
    z_l, xbc_l, dt_l = _mamba_in(x, mod, layer, None, g1, wz, wx, wd)
    z_c, xbc_c, dt_c = _mamba_in(xc, mod, layer, nb, g1, wz, wx, wd)
    h0 = jnp.zeros((x.shape[0], SSM_GROUPS, SSM_STATE, SSM_GW), F32)
    dirp = lambda d: (dtb, alog, dsk[d])
    taps = lambda d: (cw[d], cbias[d])
    yf_c, hf = _ssd(xbc_c, dt_c, h0, *dirp(0), rev=False, write_y=need_ctx, taps=taps(0))
    yf_l, _ = _ssd(xbc_l, dt_l, hf, *dirp(0), rev=False, write_y=True, taps=taps(0))
    y_c, hb = _ssd(xbc_c, dt_c, h0, *dirp(1), rev=True, write_y=need_ctx, taps=taps(1),
                   combine=yf_c if need_ctx else None, y_dtype=BF16)
    y_l, _ = _ssd(xbc_l, dt_l, hb, *dirp(1), rev=True, write_y=True, taps=taps(1), combine=yf_l, y_dtype=BF16)
    zero_b = jnp.zeros((1, D), F32)
    x = _resid_matmul(y_l, wout, zero_b, x, mod, layer, None, 2, ssm_gate=(z_l, ng))
    if need_ctx:
        xc = _resid_matmul(y_c, wout, zero_b, xc, mod, layer, nb, 2, ssm_gate=(z_c, ng))
    return x, xc


def _qkv_kernel(*refs, rope):
    if rope:
        x_ref, mod_ref, g_ref, wq_ref, wk_ref, wv_ref, cos_ref, sin_ref, q_ref, k_ref, v_ref = refs
    else:
        x_ref, mod_ref, g_ref, wq_ref, wk_ref, wv_ref, q_ref, k_ref, v_ref = refs
    gs = g_ref[...] * (1.0 + mod_ref[1:2, :])
    h = (_rms(x_ref[...]) * gs + mod_ref[0:1, :]).astype(BF16)
    tm = h.shape[0]
    if rope:
        cos, sin = cos_ref[...], sin_ref[...]
        lane = lax.broadcasted_iota(jnp.int32, (tm, 128), 1)
        first = (lane % 32) < 16

    def rot(a_ref, w_ref, scale):
        for j2 in range(D // QKV_COLS):
            wide = jnp.dot(h, w_ref[:, j2 * QKV_COLS:(j2 + 1) * QKV_COLS], preferred_element_type=F32)
            for j in range(QKV_COLS // 128):
                blk = wide[:, j * 128:(j + 1) * 128]
                if scale != 1.0:
                    blk = blk * scale
                if rope:
                    partner = jnp.where(first, pltpu.roll(blk, 112, 1), pltpu.roll(blk, 16, 1))
                    blk = blk * cos + partner * sin
                c0 = j2 * QKV_COLS + j * 128
                a_ref[:, c0:c0 + 128] = blk.astype(a_ref.dtype)

    rot(q_ref, wq_ref, DA_HEAD_DIM ** -0.5 * math.log2(math.e))
    rot(k_ref, wk_ref, 1.0)
    v_ref[...] = jnp.dot(h, wv_ref[...], preferred_element_type=F32).astype(v_ref.dtype)


def _rope_tables(s):
    lane = jnp.arange(128)
    pos = jnp.arange(s)
    coord = jnp.where(((lane % 64) // 32 == 0)[None, :], (pos // GRID_W)[:, None], (pos % GRID_W)[:, None])
    n_freq = DA_HEAD_DIM // 4
    inv = ROPE_THETA ** (-(lane % n_freq).astype(F32) / n_freq)
    ang = coord.astype(F32) * inv[None, :]
    sign = jnp.where((lane % 32) < 16, -1.0, 1.0)
    return jnp.cos(ang), jnp.sin(ang) * sign[None, :]


def _qkv(x, mod, layer, row, g, wq, wk, wv, tables):
    nb, t, _ = x.shape
    tm = _tile(t, TOKEN_TILE)
    tok = lambda: pl.BlockSpec((None, tm, D), lambda b, i: (b, i, 0))
    in_specs = [tok(), _mod_spec(layer, row), _const_spec((1, D))] + [_const_spec((D, D))] * 3
    args = [x, mod, g, wq, wk, wv]
    if tables is not None:
        in_specs += [pl.BlockSpec((tm, 128), lambda b, i: (i, 0))] * 2
        args += list(tables)
    return pl.pallas_call(
        functools.partial(_qkv_kernel, rope=tables is not None),
        grid=(nb, t // tm),
        in_specs=in_specs, out_specs=[tok()] * 3,
        out_shape=[jax.ShapeDtypeStruct((nb, t, D), BF16)] * 3,
        compiler_params=_params(("parallel", "parallel")),
    )(*args)


def _attn_kernel(*refs, nseg, lam_init):
    lam_ref, ng_ref, q_ref = refs[:3]
    kv = refs[3:3 + 2 * nseg]
    o_ref = refs[3 + 2 * nseg]
    tq = q_ref.shape[0]
    nh = q_ref.shape[1] // DA_V
    lane = lax.broadcasted_iota(jnp.int32, (tq, DA_V), 1)
    lp = lam_ref[...]
    lam = (jnp.exp(jnp.sum(lp[0:1] * lp[1:2], axis=-1, keepdims=True))
           - jnp.exp(jnp.sum(lp[2:3] * lp[3:4], axis=-1, keepdims=True)) + lam_init)

    def head_scores(hh):
        hc = slice(hh * DA_V, (hh + 1) * DA_V)
        q = q_ref[:, hc]
        zero = jnp.zeros_like(q)
        qs = [jnp.where((lane >= e * DA_HEAD_DIM) & (lane < (e + 1) * DA_HEAD_DIM), q, zero) for e in range(2)]
        return [[lax.dot_general(qe, kv[2 * i][:, hc], (((1,), (1,)), ((), ())), preferred_element_type=F32)
                 for i in range(nseg)] for qe in qs]

    def head_out(hh, scores):
        hc = slice(hh * DA_V, (hh + 1) * DA_V)
        vs = [jnp.concatenate([kv[2 * i + 1][:, hc], jnp.ones((kv[2 * i + 1].shape[0], DA_V), BF16)], axis=1)
              for i in range(nseg)]
        halves = []
        for e in range(2):
            m = scores[e][0].max(axis=-1, keepdims=True)
            for s in scores[e][1:]:
                m = jnp.maximum(m, s.max(axis=-1, keepdims=True))
            acc = jnp.zeros((tq, 2 * DA_V), F32)
            for i in range(nseg):
                p = jnp.exp2(scores[e][i] - m)
                acc = acc + jnp.dot(p.astype(BF16), vs[i], preferred_element_type=F32)
            halves.append(acc[:, :DA_V] / acc[:, DA_V:DA_V + 1])
        o = halves[0] - lam * halves[1]
        o_ref[:, hc] = (_rms(o) * ng_ref[...] * (1.0 - lam_init)).astype(o_ref.dtype)

    all_scores = [head_scores(hh) for hh in range(nh)]
    for hh in range(nh):
        head_out(hh, all_scores[hh])


def _attention(q, segs, lam_p, ng, lam_init):
    nb, t, _ = q.shape
    tq = _tile(t, ATTN_Q_TILE)
    hw = ATTN_HEADS_PER_STEP * DA_V
    in_specs = [_const_spec((4, DA_HEAD_DIM)), _const_spec((1, DA_V)),
                pl.BlockSpec((None, tq, hw), lambda b, h, i: (b, i, h))]
    args = [lam_p, ng, q]
    for k, v in segs:
        tk = k.shape[1]
        in_specs += [pl.BlockSpec((None, tk, hw), lambda b, h, i: (b, 0, h))] * 2
        args += [k, v]
    return pl.pallas_call(
        functools.partial(_attn_kernel, nseg=len(segs), lam_init=lam_init),
        grid=(nb, DA_HEADS // ATTN_HEADS_PER_STEP, t // tq),
        in_specs=in_specs,
        out_specs=pl.BlockSpec((None, tq, hw), lambda b, h, i: (b, i, h)),
        out_shape=jax.ShapeDtypeStruct((nb, t, D), BF16),
        compiler_params=_params(("parallel", "parallel", "arbitrary")),
    )(*args)


def _attn_layer(x, xc, mod, layer, nb, g1, p, need_ctx):
    wq, wk, wv, lam_p, ng, wout = p
    lam_init = 0.8 - 0.6 * math.exp(-0.3 * layer)
    q_l, k_l, v_l = _qkv(x, mod, layer, None, g1, wq, wk, wv, _rope_tables(x.shape[1]))
    q_c, k_c, v_c = _qkv(xc, mod, layer, nb, g1, wq, wk, wv, None)
    zero_b = jnp.zeros((1, D), F32)
    o_l = _attention(q_l, [(k_c, v_c), (k_l, v_l)], lam_p, ng, lam_init)
    x = _resid_matmul(o_l, wout, zero_b, x, mod, layer, None, 2)
    if need_ctx:
        o_c = _attention(q_c, [(k_c, v_c)], lam_p, ng, lam_init)
        xc = _resid_matmul(o_c, wout, zero_b, xc, mod, layer, nb, 2)
    return x, xc


def _conf_kernel(xm_ref, xp_ref, xn_ref, mod_ref, g_ref, w1_ref, b1_ref, dw_ref, db_ref, lg_ref, lb_ref,
                 w2_ref, b2_ref, o_ref, h_ref, u_ref, c_ref, sh_ref, *, tm, nt):
    t = pl.program_id(1)
    hl = CONF_HALO
    gs = g_ref[...] * (1.0 + mod_ref[1:2, :])
    shift = mod_ref[0:1, :]
    mod = lambda x: _rms(x) * gs + shift
    h_ref[...] = jnp.concatenate([mod(xp_ref[...]), mod(xm_ref[...]), mod(xn_ref[...])], axis=0).astype(BF16)
    rowid = lax.broadcasted_iota(jnp.int32, (tm + 2 * hl, 1), 0)
    valid = ((rowid >= hl) | (t > 0)) & ((rowid < hl + tm) | (t < nt - 1))

    rb, cw = 64, CONF_COLS
    pad = CONF_KERNEL // 2
    nt8 = (tm + 2 * hl) // 8
    for c0 in range(0, D, cw):
        a = jnp.dot(h_ref[...], w1_ref[c0 // cw], preferred_element_type=F32) + b1_ref[c0 // cw]
        u = jnp.where(valid, a[:, :cw] * _sigmoid(a[:, cw:]), 0.0)
        u_ref[:, c0:c0 + cw] = u
        u3 = u.reshape(nt8, 8, cw)
        for r in range(1, 8):
            sh_ref[r - 1, 8:, c0:c0 + cw] = _shift_tiles(u3, -r, 1, nt8 - 1).reshape(8 * (nt8 - 1), cw)
        for r0 in range(0, tm, rb):
            acc = jnp.broadcast_to(db_ref[:, c0:c0 + cw], (rb, cw))
            for kk in range(CONF_KERNEL):
                d = kk - pad
                r = (-d) % 8
                s = hl + d + r + r0
                src = u_ref if r == 0 else sh_ref.at[r - 1]
                acc = acc + src[s:s + rb, c0:c0 + cw] * dw_ref[kk:kk + 1, c0:c0 + cw]
            c_ref[r0:r0 + rb, c0:c0 + cw] = acc

    cv = c_ref[...]
    mu = jnp.mean(cv, axis=-1, keepdims=True)
    xc = cv - mu
    ln = xc * lax.rsqrt(jnp.mean(xc * xc, axis=-1, keepdims=True) + EPS) * lg_ref[...] + lb_ref[...]
    y = jnp.dot(_silu(ln).astype(BF16), w2_ref[...], preferred_element_type=F32) + b2_ref[...]
    o_ref[...] = xm_ref[...] + mod_ref[2:3, :] * y


def _conformer(x, mod, layer, row, g, p):
    w1, b1, dw, db, lg, lb, w2, b2 = p
    nb, t, _ = x.shape
    tm = _tile(t, 256)
    nt = t // tm
    hl = CONF_HALO
    r = tm // hl
    return pl.pallas_call(
        functools.partial(_conf_kernel, tm=tm, nt=nt),
        grid=(nb, nt),
        in_specs=[pl.BlockSpec((None, tm, D), lambda b, i: (b, i, 0)),
                  pl.BlockSpec((None, hl, D), lambda b, i: (b, jnp.maximum(i * r - 1, 0), 0)),
                  pl.BlockSpec((None, hl, D), lambda b, i: (b, jnp.minimum((i + 1) * r, t // hl - 1), 0)),
                  _mod_spec(layer, row), _const_spec((1, D)),
                  _const_spec((D // CONF_COLS, D, 2 * CONF_COLS)), _const_spec((D // CONF_COLS, 1, 2 * CONF_COLS)),
                  _const_spec((CONF_KERNEL, D)), _const_spec((1, D)), _const_spec((1, D)), _const_spec((1, D)),
                  _const_spec((D, D)), _const_spec((1, D))],
        out_specs=pl.BlockSpec((None, tm, D), lambda b, i: (b, i, 0)),
        out_shape=jax.ShapeDtypeStruct((nb, t, D), F32),
        scratch_shapes=[pltpu.VMEM((tm + 2 * hl, D), BF16), pltpu.VMEM((tm + 2 * hl, D), F32),
                        pltpu.VMEM((tm, D), F32), pltpu.VMEM((7, tm + 2 * hl, D), F32)],
        compiler_params=_params(("parallel", "parallel")),
    )(x, x, x, mod, g, w1, b1, dw, db, lg, lb, w2, b2)


def _ffn_kernel(*refs, tm, nt, final):
    if final:
        (xm_ref, xp_ref, xn_ref, mod_ref, g_ref, wu_ref, cw_ref, cb_ref, wd_ref, fg_ref,
         o_ref, h_ref, act_ref) = refs
    else:
        (xm_ref, xp_ref, xn_ref, mod_ref, g_ref, wu_ref, cw_ref, cb_ref, wd_ref,
         o_ref, h_ref, act_ref) = refs
    t = pl.program_id(1)
    hl = FFN_HALO
    gs = g_ref[...] * (1.0 + mod_ref[4:5, :])
    shift = mod_ref[3:4, :]
    mod = lambda x: _rms(x) * gs + shift
    hp = jnp.where(t > 0, mod(xp_ref[...]), 0.0)
    hn = jnp.where(t < nt - 1, mod(xn_ref[...]), 0.0)
    h_ref[...] = jnp.concatenate([hp, mod(xm_ref[...]), hn], axis=0).astype(BF16)

    nt8, nm8 = (tm + 2 * hl) // 8, tm // 8
    y, k0 = None, 0
    up = lambda j: jnp.dot(h_ref[...], wu_ref[j], preferred_element_type=F32).reshape(nt8, 8, 2 * FFN_CHUNK)
    u_next = up(0)
    for j in range(FFN_NCHUNK):
        u3 = u_next
        if j + 1 < FFN_NCHUNK:
            u_next = up(j + 1)
        w = cw_ref[j]
        cv = cb_ref[j]
        for kk in range(3):
            cv = cv + _shift_tiles(u3, kk - 1, 1, nm8) * w[kk:kk + 1, :]
        cv = cv.reshape(tm, 2 * FFN_CHUNK)
        act_ref[:, j * FFN_CHUNK:(j + 1) * FFN_CHUNK] = (_silu(cv[:, :FFN_CHUNK]) * cv[:, FFN_CHUNK:]).astype(BF16)
        if (j + 1) % FFN_DOWN_EVERY == 0 or j == FFN_NCHUNK - 1:
            k1 = (j + 1) * FFN_CHUNK
            part = jnp.dot(act_ref[:, k0:k1], wd_ref[k0:k1, :], preferred_element_type=F32)
            y = part if y is None else y + part
            k0 = k1
    out = xm_ref[...] + mod_ref[5:6, :] * y
    if final:
        out = _rms(out) * fg_ref[...]
    o_ref[...] = out


def _ffn(x, mod, layer, row, g, p, final_g=None):
    wu, cw, cb, wd = p
    nb, t, _ = x.shape
    tm = _tile(t, TOKEN_TILE)
    nt = t // tm
    hl = FFN_HALO
    r = tm // hl
    final = final_g is not None
    in_specs = [pl.BlockSpec((None, tm, D), lambda b, i: (b, i, 0)),
                pl.BlockSpec((None, hl, D), lambda b, i: (b, jnp.maximum(i * r - 1, 0), 0)),
                pl.BlockSpec((None, hl, D), lambda b, i: (b, jnp.minimum((i + 1) * r, t // hl - 1), 0)),
                _mod_spec(layer, row), _const_spec((1, D)),
                _const_spec((FFN_NCHUNK, D, 2 * FFN_CHUNK)), _const_spec((FFN_NCHUNK, 3, 2 * FFN_CHUNK)),
                _const_spec((FFN_NCHUNK, 1, 2 * FFN_CHUNK)), _const_spec((FFN_HIDDEN, D))]
    args = [x, x, x, mod, g, wu, cw, cb, wd]
    if final:
        in_specs.append(_const_spec((1, D)))
        args.append(final_g)
    return pl.pallas_call(
        functools.partial(_ffn_kernel, tm=tm, nt=nt, final=final),
        grid=(nb, nt),
        in_specs=in_specs,
        out_specs=pl.BlockSpec((None, tm, D), lambda b, i: (b, i, 0)),
        out_shape=jax.ShapeDtypeStruct((nb, t, D), F32),
        scratch_shapes=[pltpu.VMEM((tm + 2 * hl, D), BF16), pltpu.VMEM((tm, FFN_HIDDEN), BF16)],
        compiler_params=_params(("parallel", "parallel")),
    )(*args)


def _pair_cols(a, nchunk, width):
    lead = a.shape[:-1]
    a = a.reshape(lead + (2, nchunk, width))
    a = jnp.moveaxis(a, -2, 0)
    return a.reshape((nchunk,) + lead + (2 * width,))


def _ffn_weights(w_up, conv_w, conv_b, w_down):
    pair = lambda a: _pair_cols(a, FFN_NCHUNK, FFN_CHUNK)
    return (pair(w_up).astype(BF16), pair(conv_w), pair(conv_b[None, :]),
            w_down.astype(BF16))


def kernel(x, c, ctx, c_ctx, mod_w, mod_b, norm1_g, norm2_g, ffn_w_up, ffn_conv_w, ffn_conv_b, ffn_w_down,
           ssm_w_in, ssm_conv_w, ssm_conv_b, ssm_dt_bias, ssm_a_log, ssm_d, ssm_norm_g, ssm_w_out, attn_w_in,
           attn_lambda, attn_norm_g, attn_w_out, conf_w_pw1, conf_b_pw1, conf_dw_w, conf_dw_b, conf_ln_g,
           conf_ln_b, conf_w_pw2, conf_b_pw2, final_g):
    depth = mod_w.shape[0]
    nb = x.shape[0]
    mod = _mod_table(c, c_ctx, mod_w, mod_b)
    xc = ctx
    row = lambda a: a.reshape(1, -1)
    pad_dt = lambda a: jnp.pad(a.reshape(1, -1), ((0, 0), (0, DT_PAD - 2 * SSM_HEADS)))
    for i in range(depth):
        kind, j = i % N_MIXERS, i // N_MIXERS
        need_ctx = i < depth - 1
        g1 = row(norm1_g[i])
        if kind == 0:
            w_in = ssm_w_in[j]
            o_x, o_dt = D_INNER, D_INNER + SSM_CONV_CH
            wd = jnp.pad(w_in[:, o_dt:], ((0, 0), (0, DT_PAD - 2 * SSM_HEADS)))
            p = (w_in[:, :o_x].astype(BF16), w_in[:, o_x:o_dt].astype(BF16), wd.astype(BF16),
                 ssm_conv_w[j], ssm_conv_b[j][:, None, :], pad_dt(ssm_dt_bias[j]), pad_dt(ssm_a_log[j]),
                 jnp.repeat(ssm_d[j], SSM_HEAD_DIM, axis=-1)[:, None, :], row(ssm_norm_g[j]),
                 ssm_w_out[j].astype(BF16))
            x, xc = _mamba_layer(x, xc, mod, i, nb, g1, p, need_ctx)
        elif kind == 1:
            w_in = attn_w_in[j].astype(BF16)
            p = (w_in[:, :D], w_in[:, D:2 * D], w_in[:, 2 * D:], attn_lambda[j], row(attn_norm_g[j]),
                 attn_w_out[j].astype(BF16))
            x, xc = _attn_layer(x, xc, mod, i, nb, g1, p, need_ctx)
        else:
            p = (_pair_cols(conf_w_pw1[j], D // CONF_COLS, CONF_COLS).astype(BF16),
                 _pair_cols(row(conf_b_pw1[j]), D // CONF_COLS, CONF_COLS), conf_dw_w[j], row(conf_dw_b[j]),
                 row(conf_ln_g[j]), row(conf_ln_b[j]), conf_w_pw2[j].astype(BF16), row(conf_b_pw2[j]))
            x = _conformer(x, mod, i, None, g1, p)
            if need_ctx:
                xc = _conformer(xc, mod, i, nb, g1, p)
        fp = _ffn_weights(ffn_w_up[i], ffn_conv_w[i], ffn_conv_b[i], ffn_w_down[i])
        g2 = row(norm2_g[i])
        x = _ffn(x, mod, i, None, g2, fp, final_g=row(final_g) if i == depth - 1 else None)
        if need_ctx:
            xc = _ffn(xc, mod, i, nb, g2, fp)
    return x
```

```python
import functools
import math

import jax
import jax.numpy as jnp
from jax import lax
from jax.experimental import pallas as pl
from jax.experimental.pallas import tpu as pltpu

F32 = jnp.float32
BF16 = jnp.bfloat16
HIGHEST = lax.Precision.HIGHEST

D = 1024
N_MOD = 6
N_MIXERS = 3
EPS = 1e-6
GRID_W = 64
ROPE_THETA = 10000.0

SSM_HEAD_DIM = 64
SSM_HEADS = 32
SSM_GROUPS = 8
SSM_HPG = SSM_HEADS // SSM_GROUPS
SSM_STATE = 128
SSM_CONV = 4
SSM_CHUNK = 128
D_INNER = SSM_HEADS * SSM_HEAD_DIM
SSM_GW = SSM_HPG * SSM_HEAD_DIM
SSM_BC = SSM_GROUPS * SSM_STATE
SSM_CONV_CH = D_INNER + 2 * SSM_BC
DT_PAD = 128
SSD_HALO = 16

DA_HEAD_DIM = 64
DA_HEADS = D // (2 * DA_HEAD_DIM)
DA_V = 2 * DA_HEAD_DIM

CONF_KERNEL = 31
CONF_HALO = 16
CONF_COLS = 256

FFN_HIDDEN = 2816
FFN_CHUNK = 256
FFN_NCHUNK = FFN_HIDDEN // FFN_CHUNK
FFN_HALO = 8
FFN_LOOKAHEAD = 3
FFN_DOWN_EVERY = 4

VMEM_LIMIT = 56 * 1024 * 1024
TOKEN_TILE = 512
RESID_ROW_BLOCKS = 2
ATTN_Q_TILE = 512
ATTN_HEADS_PER_STEP = 2
QKV_COLS = 512


def _sigmoid(x):
    return 1.0 / (1.0 + jnp.exp2(x * -math.log2(math.e)))


def _silu(x):
    return x * _sigmoid(x)


def _softplus(x):
    return jnp.maximum(x, 0.0) + jnp.log(1.0 + jnp.exp(-jnp.abs(x)))


def _rms(x):
    return x * lax.rsqrt(jnp.mean(x * x, axis=-1, keepdims=True) + EPS)


def _shift_tiles(x3, s, first, count):
    if s == 0:
        return x3[first:first + count]
    sub = lax.broadcasted_iota(jnp.int32, (count,) + x3.shape[1:], 1)
    rk = pltpu.roll(x3, (-s) % 8, 1)
    if s < 0:
        return jnp.where(sub >= -s, rk[first:first + count], rk[first - 1:first - 1 + count])
    return jnp.where(sub < 8 - s, rk[first:first + count], rk[first + 1:first + 1 + count])


def _dot01(m01, x, *, left):
    out, rest = None, x
    for _ in range(3):
        piece = rest.astype(BF16)
        part = jnp.dot(m01, piece, preferred_element_type=F32) if left else \
            jnp.dot(piece, m01, preferred_element_type=F32)
        out = part if out is None else out + part
        rest = rest - piece.astype(F32)
    return out


def _params(sem, vmem=VMEM_LIMIT):
    return pltpu.CompilerParams(dimension_semantics=sem, vmem_limit_bytes=vmem)


def _mod_spec(layer, row):
    if row is None:
        return pl.BlockSpec((None, None, N_MOD, D), lambda b, *_: (layer, b, 0, 0))
    return pl.BlockSpec((None, None, N_MOD, D), lambda b, *_: (layer, row, 0, 0))


def _const_spec(shape):
    nd = len(shape)
    return pl.BlockSpec(shape, lambda *_: (0,) * nd)


def _tile(n, pref):
    return pref if n % pref == 0 else n


def _mod_kernel(s_ref, w_ref, b_ref, o_ref):
    s = _silu(s_ref[...])
    o_ref[...] = jnp.dot(s, w_ref[...], preferred_element_type=F32, precision=HIGHEST) + b_ref[...]


def _mod_table(c, c_ctx, mod_w, mod_b):
    depth = mod_w.shape[0]
    nb = c.shape[0]
    rows = -(-(nb + 1) // 8) * 8
    s = jnp.zeros((rows, D), F32).at[:nb].set(c).at[nb].set(c_ctx)
    tn = 1536
    out = pl.pallas_call(
        _mod_kernel,
        grid=(depth, N_MOD * D // tn),
        in_specs=[pl.BlockSpec((rows, D), lambda i, n: (0, 0)),
                  pl.BlockSpec((None, D, tn), lambda i, n: (i, 0, n)),
                  pl.BlockSpec((None, 1, tn), lambda i, n: (i, 0, n))],
        out_specs=pl.BlockSpec((None, rows, tn), lambda i, n: (i, 0, n)),
        out_shape=jax.ShapeDtypeStruct((depth, rows, N_MOD * D), F32),
        compiler_params=_params(("arbitrary", "arbitrary")),
    )(s, mod_w, mod_b.reshape(depth, 1, N_MOD * D))
    return out.reshape(depth, rows, N_MOD, D)


def _mamba_in_kernel(x_ref, mod_ref, g_ref, wz_ref, wx_ref, wd_ref, z_ref, xbc_ref, dt_ref):
    gs = g_ref[...] * (1.0 + mod_ref[1:2, :])
    h = (_rms(x_ref[...]) * gs + mod_ref[0:1, :]).astype(BF16)
    for n in range(0, D_INNER, 1024):
        z_ref[:, n:n + 1024] = jnp.dot(h, wz_ref[:, n:n + 1024], preferred_element_type=F32).astype(z_ref.dtype)
    for n in range(0, SSM_CONV_CH, 1024):
        xbc_ref[:, n:n + 1024] = jnp.dot(h, wx_ref[:, n:n + 1024],
                                         preferred_element_type=F32).astype(xbc_ref.dtype)
    dt_ref[...] = jnp.dot(h, wd_ref[...], preferred_element_type=F32)


def _mamba_in(x, mod, layer, row, g, wz, wx, wd):
    nb, t, _ = x.shape
    tm = _tile(t, TOKEN_TILE)
    tok = lambda w: pl.BlockSpec((None, tm, w), lambda b, i: (b, i, 0))
    return pl.pallas_call(
        _mamba_in_kernel,
        grid=(nb, t // tm),
        in_specs=[tok(D), _mod_spec(layer, row), _const_spec((1, D)),
                  _const_spec((D, D_INNER)), _const_spec((D, SSM_CONV_CH)), _const_spec((D, DT_PAD))],
        out_specs=[tok(D_INNER), tok(SSM_CONV_CH), tok(DT_PAD)],
        out_shape=[jax.ShapeDtypeStruct((nb, t, D_INNER), BF16),
                   jax.ShapeDtypeStruct((nb, t, SSM_CONV_CH), BF16),
                   jax.ShapeDtypeStruct((nb, t, DT_PAD), F32)],
        compiler_params=_params(("parallel", "parallel")),
    )(x, mod, g, wz, wx, wd)


def _ssd_kernel(*refs, rev, nc, write_y, combine):
    it = iter(refs)
    u_ref, xh_ref, cw_ref, cbias_ref = (next(it) for _ in range(4))
    dt_ref, dtb_ref, alog_ref, dsk_ref, e_ref, h0_ref = (next(it) for _ in range(6))
    if combine:
        yf_ref = next(it)
    y_ref = next(it) if write_y else None
    hfin_ref = next(it)
    ht_ref = next(it)

    k = pl.program_id(1)
    ch = SSM_CHUNK
    off = SSM_HEADS if rev else 0

    @pl.when(k == 0)
    def _():
        ht_ref[...] = h0_ref[...]

    c = (nc - 1 - k) if rev else k
    side_on = (c < nc - 1) if rev else (c > 0)

    def load(c0, w):
        main = u_ref[:, c0:c0 + w].astype(F32)
        side = jnp.where(side_on, xh_ref[:, c0:c0 + w].astype(F32), 0.0)
        buf = jnp.concatenate([main, side] if rev else [side, main], axis=0)
        x3 = buf.reshape((ch + SSD_HALO) // 8, 8, w)
        first = 0 if rev else SSD_HALO // 8
        acc = cbias_ref[:, c0:c0 + w]
        for kk in range(SSM_CONV):
            s = (SSM_CONV - 1 - kk) * (1 if rev else -1)
            acc = acc + _shift_tiles(x3, s, first, ch // 8) * cw_ref[kk:kk + 1, c0:c0 + w]
        return _silu(acc).reshape(ch, w)

    dtv = _softplus(dt_ref[...] + dtb_ref[...])
    la = dtv * (-jnp.exp(alog_ref[...]))
    ri = lax.broadcasted_iota(jnp.int32, (ch, ch), 0)
    ci = lax.broadcasted_iota(jnp.int32, (ch, ch), 1)
    keep = (ci >= ri) if rev else (ci <= ri)
    causal = jnp.where(keep, 0.0, -jnp.inf)
    acum = _dot01(jnp.where(keep, 1.0, 0.0).astype(BF16), la, left=True)
    last = 0 if rev else ch - 1
    total = acum[last:last + 1, :]
    wcol = jnp.exp(total - acum) * dtv
    offv = jnp.exp(acum)
    acum2 = acum * math.log2(math.e)
    acum2_t = acum2.T
    dt_t = dtv.T

    lane = lax.broadcasted_iota(jnp.int32, (ch, SSM_GW), 1)

    wexp = jnp.dot(wcol.astype(BF16), e_ref[...], preferred_element_type=F32)
    sexp = jnp.dot(offv.astype(BF16), e_ref[...], preferred_element_type=F32)
    cdec = _dot01(e_ref[...], jnp.broadcast_to(offv[last:last + 1, :], (8, DT_PAD)), left=False)[0:1, :]

    for g in range(SSM_GROUPS):
        gc = slice(g * SSM_GW, (g + 1) * SSM_GW)
        xg = load(g * SSM_GW, SSM_GW)
        xg16 = xg.astype(BF16)
        bg16 = load(D_INNER + g * SSM_STATE, SSM_STATE).astype(BF16)
        xw = (xg * wexp[:, gc]).astype(BF16)
        st_t = lax.dot_general(bg16, xw, (((0,), (0,)), ((), ())), preferred_element_type=F32)
        sg = sexp[:, gc]
        ht_old = ht_ref[g]
        if write_y:
            cg16 = load(D_INNER + SSM_BC + g * SSM_STATE, SSM_STATE).astype(BF16)
            cbm = lax.dot_general(cg16, bg16, (((1,), (1,)), ((), ())), preferred_element_type=F32)
            yoff = jnp.dot(cg16, ht_old.astype(BF16), preferred_element_type=F32)
            ms, xs = [], []
            for j in range(SSM_HPG):
                col = off + SSM_HPG * g + j
                seg = acum2[:, col:col + 1] - acum2_t[col:col + 1, :]
                lmat = jnp.exp2(seg + causal)
                ms.append((cbm * lmat * dt_t[col:col + 1, :]).astype(BF16))
                in_head = (lane >= j * SSM_HEAD_DIM) & (lane < (j + 1) * SSM_HEAD_DIM)
                xs.append(jnp.where(in_head, xg16, jnp.zeros_like(xg16)))
            ydiag = jnp.dot(jnp.concatenate(ms, axis=1), jnp.concatenate(xs, axis=0),
                            preferred_element_type=F32)
            y = ydiag + yoff * sg + dsk_ref[:, g * SSM_GW:(g + 1) * SSM_GW] * xg
            cols = slice(g * SSM_GW, (g + 1) * SSM_GW)
            if combine:
                y = y + yf_ref[:, cols].astype(F32)
            y_ref[:, cols] = y.astype(y_ref.dtype)
        ht_ref[g] = ht_old * cdec[:, gc] + st_t

    @pl.when(k == nc - 1)
    def _():
        hfin_ref[...] = ht_ref[...]


def _ssd(u, dt, h0, dt_bias, a_log, d_skip, taps, *, rev, write_y, combine=None, y_dtype=F32):
    nb, t, _ = u.shape
    nc = t // SSM_CHUNK
    hb = SSM_CHUNK // SSD_HALO
    if rev:
        cidx = lambda k: nc - 1 - k
        hidx = lambda k: jnp.minimum((nc - k) * hb, t // SSD_HALO - 1)
    else:
        cidx = lambda k: k
        hidx = lambda k: jnp.maximum(k * hb - 1, 0)
    chunk = lambda w: pl.BlockSpec((None, SSM_CHUNK, w), lambda b, k: (b, cidx(k), 0))
    state = pl.BlockSpec((None, SSM_GROUPS, SSM_STATE, SSM_GW), lambda b, k: (b, 0, 0, 0))
    in_specs = [chunk(SSM_CONV_CH), pl.BlockSpec((None, SSD_HALO, SSM_CONV_CH), lambda b, k: (b, hidx(k), 0)),
                _const_spec((SSM_CONV, SSM_CONV_CH)), _const_spec((1, SSM_CONV_CH))]
    args = [u, u, taps[0], taps[1]]
    scratch = [pltpu.VMEM((SSM_GROUPS, SSM_STATE, SSM_GW), F32)]
    off = SSM_HEADS if rev else 0
    spread = jnp.repeat(jnp.eye(DT_PAD, dtype=BF16)[:, off:off + SSM_HEADS], SSM_HEAD_DIM, axis=1)
    in_specs += [chunk(DT_PAD), _const_spec((1, DT_PAD)), _const_spec((1, DT_PAD)), _const_spec((1, D_INNER)),
                 _const_spec((DT_PAD, D_INNER)), state]
    args += [dt, dt_bias, a_log, d_skip, spread, h0]
    out_specs, out_shape = [], []
    if combine is not None:
        in_specs.append(chunk(D_INNER))
        args.append(combine)
    if write_y:
        out_specs.append(chunk(D_INNER))
        out_shape.append(jax.ShapeDtypeStruct((nb, t, D_INNER), y_dtype))
    out_specs.append(state)
    out_shape.append(jax.ShapeDtypeStruct((nb, SSM_GROUPS, SSM_STATE, SSM_GW), F32))
    outs = pl.pallas_call(
        functools.partial(_ssd_kernel, rev=rev, nc=nc, write_y=write_y, combine=combine is not None),
        grid=(nb, nc),
        in_specs=in_specs, out_specs=out_specs, out_shape=out_shape,
        scratch_shapes=scratch,
        compiler_params=_params(("parallel", "arbitrary")),
    )(*args)
    return outs if write_y else (None, outs[0])


def _resid_kernel(*refs, gate_row, ssm_gate):
    if not ssm_gate:
        a_ref, w_ref, b_ref, x_ref, mod_ref, o_ref = refs
        y = jnp.dot(a_ref[...], w_ref[...], preferred_element_type=F32) + b_ref[...]
        o_ref[...] = x_ref[...] + mod_ref[gate_row:gate_row + 1, :] * y
        return
    a_ref, z_ref, ng_ref, w_ref, b_ref, x_ref, mod_ref, o_ref = refs
    tm = a_ref.shape[0]
    rb = tm // RESID_ROW_BLOCKS
    for r0 in range(0, tm, rb):
        parts = []
        for g in range(SSM_GROUPS):
            cols = slice(g * SSM_GW, (g + 1) * SSM_GW)
            yg = a_ref[r0:r0 + rb, cols].astype(F32) * _silu(z_ref[r0:r0 + rb, cols].astype(F32))
            parts.append((_rms(yg) * ng_ref[:, cols]).astype(BF16))
        y = jnp.dot(jnp.concatenate(parts, axis=1), w_ref[...], preferred_element_type=F32) + b_ref[...]
        o_ref[r0:r0 + rb, :] = x_ref[r0:r0 + rb, :] + mod_ref[gate_row:gate_row + 1, :] * y


def _resid_matmul(a, w, bias, x, mod, layer, row, gate_row, ssm_gate=None):
    nb, t, kdim = a.shape
    tm = _tile(t, TOKEN_TILE)
    tok = lambda wd: pl.BlockSpec((None, tm, wd), lambda b, i: (b, i, 0))
    in_specs, args = [tok(kdim)], [a]
    if ssm_gate is not None:
        in_specs += [tok(kdim), _const_spec((1, kdim))]
        args += list(ssm_gate)
    in_specs += [_const_spec((kdim, D)), _const_spec((1, D)), tok(D), _mod_spec(layer, row)]
    args += [w, bias, x, mod]
    return pl.pallas_call(
        functools.partial(_resid_kernel, gate_row=gate_row, ssm_gate=ssm_gate is not None),
        grid=(nb, t // tm),
        in_specs=in_specs,
        out_specs=tok(D),
        out_shape=jax.ShapeDtypeStruct((nb, t, D), F32),
        compiler_params=_params(("parallel", "parallel")),
    )(*args)


def _mamba_layer(x, xc, mod, layer, nb, g1, p, need_ctx):
    wz, wx, wd, cw, cbias, dtb, alog, dsk, ng, wout = p
    z_l, xbc_l, dt_l = _mamba_in(x, mod, layer, None, g1, wz, wx, wd)
    z_c, xbc_c, dt_c = _mamba_in(xc, mod, layer, nb, g1, wz, wx, wd)
    h0 = jnp.zeros((x.shape[0], SSM_GROUPS, SSM_STATE, SSM_GW), F32)
    dirp = lambda d: (dtb, alog, dsk[d])
    taps = lambda d: (cw[d], cbias[d])
    yf_c, hf = _ssd(xbc_c, dt_c, h0, *dirp(0), rev=False, write_y=need_ctx, taps=taps(0))
    yf_l, _ = _ssd(xbc_l, dt_l, hf, *dirp(0), rev=False, write_y=True, taps=taps(0))
    y_c, hb = _ssd(xbc_c, dt_c, h0, *dirp(1), rev=True, write_y=need_ctx, taps=taps(1),
                   combine=yf_c if need_ctx else None, y_dtype=BF16)
    y_l, _ = _ssd(xbc_l, dt_l, hb, *dirp(1), rev=True, write_y=True, taps=taps(1), combine=yf_l, y_dtype=BF16)
    zero_b = jnp.zeros((1, D), F32)
    x = _resid_matmul(y_l, wout, zero_b, x, mod, layer, None, 2, ssm_gate=(z_l, ng))
    if need_ctx:
        xc = _resid_matmul(y_c, wout, zero_b, xc, mod, layer, nb, 2, ssm_gate=(z_c, ng))
    return x, xc


def _qkv_kernel(*refs, rope):
    if rope:
        x_ref, mod_ref, g_ref, wq_ref, wk_ref, wv_ref, cos_ref, sin_ref, q_ref, k_ref, v_ref = refs
    else:
        x_ref, mod_ref, g_ref, wq_ref, wk_ref, wv_ref, q_ref, k_ref, v_ref = refs
    gs = g_ref[...] * (1.0 + mod_ref[1:2, :])
    h = (_rms(x_ref[...]) * gs + mod_ref[0:1, :]).astype(BF16)
    tm = h.shape[0]
    if rope:
        cos, sin = cos_ref[...], sin_ref[...]
        lane = lax.broadcasted_iota(jnp.int32, (tm, 128), 1)
        first = (lane % 32) < 16

    def rot(a_ref, w_ref, scale):
        for j2 in range(D // QKV_COLS):
            wide = jnp.dot(h, w_ref[:, j2 * QKV_COLS:(j2 + 1) * QKV_COLS], preferred_element_type=F32)
            for j in range(QKV_COLS // 128):
                blk = wide[:, j * 128:(j + 1) * 128]
                if scale != 1.0:
                    blk = blk * scale
                if rope:
                    partner = jnp.where(first, pltpu.roll(blk, 112, 1), pltpu.roll(blk, 16, 1))
                    blk = blk * cos + partner * sin
                c0 = j2 * QKV_COLS + j * 128
                a_ref[:, c0:c0 + 128] = blk.astype(a_ref.dtype)

    rot(q_ref, wq_ref, DA_HEAD_DIM ** -0.5 * math.log2(math.e))
    rot(k_ref, wk_ref, 1.0)
    v_ref[...] = jnp.dot(h, wv_ref[...], preferred_element_type=F32).astype(v_ref.dtype)


def _rope_tables(s):
    lane = jnp.arange(128)
    pos = jnp.arange(s)
    coord = jnp.where(((lane % 64) // 32 == 0)[None, :], (pos // GRID_W)[:, None], (pos % GRID_W)[:, None])
    n_freq = DA_HEAD_DIM // 4
    inv = ROPE_THETA ** (-(lane % n_freq).astype(F32) / n_freq)
    ang = coord.astype(F32) * inv[None, :]
    sign = jnp.where((lane % 32) < 16, -1.0, 1.0)
    return jnp.cos(ang), jnp.sin(ang) * sign[None, :]


def _qkv(x, mod, layer, row, g, wq, wk, wv, tables):
    nb, t, _ = x.shape
    tm = _tile(t, TOKEN_TILE)
    tok = lambda: pl.BlockSpec((None, tm, D), lambda b, i: (b, i, 0))
    in_specs = [tok(), _mod_spec(layer, row), _const_spec((1, D))] + [_const_spec((D, D))] * 3
    args = [x, mod, g, wq, wk, wv]
    if tables is not None:
        in_specs += [pl.BlockSpec((tm, 128), lambda b, i: (i, 0))] * 2
        args += list(tables)
    return pl.pallas_call(
        functools.partial(_qkv_kernel, rope=tables is not None),
        grid=(nb, t // tm),
        in_specs=in_specs, out_specs=[tok()] * 3,
        out_shape=[jax.ShapeDtypeStruct((nb, t, D), BF16)] * 3,
        compiler_params=_params(("parallel", "parallel")),
    )(*args)


def _attn_kernel(*refs, nseg, lam_init):
    lam_ref, ng_ref, q_ref = refs[:3]
    kv = refs[3:3 + 2 * nseg]
    o_ref = refs[3 + 2 * nseg]
    tq = q_ref.shape[0]
    nh = q_ref.shape[1] // DA_V
    lane = lax.broadcasted_iota(jnp.int32, (tq, DA_V), 1)
    lp = lam_ref[...]
    lam = (jnp.exp(jnp.sum(lp[0:1] * lp[1:2], axis=-1, keepdims=True))
           - jnp.exp(jnp.sum(lp[2:3] * lp[3:4], axis=-1, keepdims=True)) + lam_init)

    def head_scores(hh):
        hc = slice(hh * DA_V, (hh + 1) * DA_V)
        q = q_ref[:, hc]
        zero = jnp.zeros_like(q)
        qs = [jnp.where((lane >= e * DA_HEAD_DIM) & (lane < (e + 1) * DA_HEAD_DIM), q, zero) for e in range(2)]
        return [[lax.dot_general(qe, kv[2 * i][:, hc], (((1,), (1,)), ((), ())), preferred_element_type=F32)
                 for i in range(nseg)] for qe in qs]

    def head_out(hh, scores):
        hc = slice(hh * DA_V, (hh + 1) * DA_V)
        vs = [jnp.concatenate([kv[2 * i + 1][:, hc], jnp.ones((kv[2 * i + 1].shape[0], DA_V), BF16)], axis=1)
              for i in range(nseg)]
        halves = []
        for e in range(2):
            m = scores[e][0].max(axis=-1, keepdims=True)
            for s in scores[e][1:]:
                m = jnp.maximum(m, s.max(axis=-1, keepdims=True))
            acc = jnp.zeros((tq, 2 * DA_V), F32)
            for i in range(nseg):
                p = jnp.exp2(scores[e][i] - m)
                acc = acc + jnp.dot(p.astype(BF16), vs[i], preferred_element_type=F32)
            halves.append(acc[:, :DA_V] / acc[:, DA_V:DA_V + 1])
        o = halves[0] - lam * halves[1]
        o_ref[:, hc] = (_rms(o) * ng_ref[...] * (1.0 - lam_init)).astype(o_ref.dtype)

    all_scores = [head_scores(hh) for hh in range(nh)]
    for hh in range(nh):
        head_out(hh, all_scores[hh])


def _attention(q, segs, lam_p, ng, lam_init):
    nb, t, _ = q.shape
    tq = _tile(t, ATTN_Q_TILE)
    hw = ATTN_HEADS_PER_STEP * DA_V
    in_specs = [_const_spec((4, DA_HEAD_DIM)), _const_spec((1, DA_V)),
                pl.BlockSpec((None, tq, hw), lambda b, h, i: (b, i, h))]
    args = [lam_p, ng, q]
    for k, v in segs:
        tk = k.shape[1]
        in_specs += [pl.BlockSpec((None, tk, hw), lambda b, h, i: (b, 0, h))] * 2
        args += [k, v]
    return pl.pallas_call(
        functools.partial(_attn_kernel, nseg=len(segs), lam_init=lam_init),
        grid=(nb, DA_HEADS // ATTN_HEADS_PER_STEP, t // tq),
        in_specs=in_specs,
        out_specs=pl.BlockSpec((None, tq, hw), lambda b, h, i: (b, i, h)),
        out_shape=jax.ShapeDtypeStruct((nb, t, D), BF16),
        compiler_params=_params(("parallel", "parallel", "arbitrary")),
    )(*args)


def _attn_layer(x, xc, mod, layer, nb, g1, p, need_ctx):
    wq, wk, wv, lam_p, ng, wout = p
    lam_init = 0.8 - 0.6 * math.exp(-0.3 * layer)
    q_l, k_l, v_l = _qkv(x, mod, layer, None, g1, wq, wk, wv, _rope_tables(x.shape[1]))
    q_c, k_c, v_c = _qkv(xc, mod, layer, nb, g1, wq, wk, wv, None)
    zero_b = jnp.zeros((1, D), F32)
    o_l = _attention(q_l, [(k_c, v_c), (k_l, v_l)], lam_p, ng, lam_init)
    x = _resid_matmul(o_l, wout, zero_b, x, mod, layer, None, 2)
    if need_ctx:
        o_c = _attention(q_c, [(k_c, v_c)], lam_p, ng, lam_init)
        xc = _resid_matmul(o_c, wout, zero_b, xc, mod, layer, nb, 2)
    return x, xc


def _conf_kernel(xm_ref, xp_ref, xn_ref, mod_ref, g_ref, w1_ref, b1_ref, dw_ref, db_ref, lg_ref, lb_ref,
                 w2_ref, b2_ref, o_ref, h_ref, u_ref, c_ref, sh_ref, *, tm, nt):
    t = pl.program_id(1)
    hl = CONF_HALO
    gs = g_ref[...] * (1.0 + mod_ref[1:2, :])
    shift = mod_ref[0:1, :]
    mod = lambda x: _rms(x) * gs + shift
    h_ref[...] = jnp.concatenate([mod(xp_ref[...]), mod(xm_ref[...]), mod(xn_ref[...])], axis=0).astype(BF16)
    rowid = lax.broadcasted_iota(jnp.int32, (tm + 2 * hl, 1), 0)
    valid = ((rowid >= hl) | (t > 0)) & ((rowid < hl + tm) | (t < nt - 1))

    rb, cw = 64, CONF_COLS
    pad = CONF_KERNEL // 2
    nt8 = (tm + 2 * hl) // 8
    glu_in = lambda cb: jnp.dot(h_ref[...], w1_ref[cb], preferred_element_type=F32) + b1_ref[cb]
    a_next = glu_in(0)
    for c0 in range(0, D, cw):
        a = a_next
        if c0 + cw < D:
            a_next = glu_in(c0 // cw + 1)
        u = jnp.where(valid, a[:, :cw] * _sigmoid(a[:, cw:]), 0.0)
        u_ref[:, c0:c0 + cw] = u
        u3 = u.reshape(nt8, 8, cw)
        for r in range(1, 8):
            sh_ref[r - 1, 8:, c0:c0 + cw] = _shift_tiles(u3, -r, 1, nt8 - 1).reshape(8 * (nt8 - 1), cw)
        for r0 in range(0, tm, rb):
            acc = jnp.broadcast_to(db_ref[:, c0:c0 + cw], (rb, cw))
            for kk in range(CONF_KERNEL):
                d = kk - pad
                r = (-d) % 8
                s = hl + d + r + r0
                src = u_ref if r == 0 else sh_ref.at[r - 1]
                acc = acc + src[s:s + rb, c0:c0 + cw] * dw_ref[kk:kk + 1, c0:c0 + cw]
            c_ref[r0:r0 + rb, c0:c0 + cw] = acc

    cv = c_ref[...]
    mu = jnp.mean(cv, axis=-1, keepdims=True)
    xc = cv - mu
    ln = xc * lax.rsqrt(jnp.mean(xc * xc, axis=-1, keepdims=True) + EPS) * lg_ref[...] + lb_ref[...]
    y = jnp.dot(_silu(ln).astype(BF16), w2_ref[...], preferred_element_type=F32) + b2_ref[...]
    o_ref[...] = xm_ref[...] + mod_ref[2:3, :] * y


def _conformer(x, mod, layer, row, g, p):
    w1, b1, dw, db, lg, lb, w2, b2 = p
    nb, t, _ = x.shape
    tm = _tile(t, 256)
    nt = t // tm
    hl = CONF_HALO
    r = tm // hl
    return pl.pallas_call(
        functools.partial(_conf_kernel, tm=tm, nt=nt),
        grid=(nb, nt),
        in_specs=[pl.BlockSpec((None, tm, D), lambda b, i: (b, i, 0)),
                  pl.BlockSpec((None, hl, D), lambda b, i: (b, jnp.maximum(i * r - 1, 0), 0)),
                  pl.BlockSpec((None, hl, D), lambda b, i: (b, jnp.minimum((i + 1) * r, t // hl - 1), 0)),
                  _mod_spec(layer, row), _const_spec((1, D)),
                  _const_spec((D // CONF_COLS, D, 2 * CONF_COLS)), _const_spec((D // CONF_COLS, 1, 2 * CONF_COLS)),
                  _const_spec((CONF_KERNEL, D)), _const_spec((1, D)), _const_spec((1, D)), _const_spec((1, D)),
                  _const_spec((D, D)), _const_spec((1, D))],
        out_specs=pl.BlockSpec((None, tm, D), lambda b, i: (b, i, 0)),
        out_shape=jax.ShapeDtypeStruct((nb, t, D), F32),
        scratch_shapes=[pltpu.VMEM((tm + 2 * hl, D), BF16), pltpu.VMEM((tm + 2 * hl, D), F32),
                        pltpu.VMEM((tm, D), F32), pltpu.VMEM((7, tm + 2 * hl, D), F32)],
        compiler_params=_params(("parallel", "parallel")),
    )(x, x, x, mod, g, w1, b1, dw, db, lg, lb, w2, b2)


def _ffn_kernel(*refs, tm, nt, final):
    if final:
        (xm_ref, xp_ref, xn_ref, mod_ref, g_ref, wu_ref, cw_ref, cb_ref, wd_ref, fg_ref,
         o_ref, h_ref, act_ref) = refs
    else:
        (xm_ref, xp_ref, xn_ref, mod_ref, g_ref, wu_ref, cw_ref, cb_ref, wd_ref,
         o_ref, h_ref, act_ref) = refs
    t = pl.program_id(1)
    hl = FFN_HALO
    gs = g_ref[...] * (1.0 + mod_ref[4:5, :])
    shift = mod_ref[3:4, :]
    mod = lambda x: _rms(x) * gs + shift
    hp = jnp.where(t > 0, mod(xp_ref[...]), 0.0)
    hn = jnp.where(t < nt - 1, mod(xn_ref[...]), 0.0)
    h_ref[...] = jnp.concatenate([hp, mod(xm_ref[...]), hn], axis=0).astype(BF16)

    nt8, nm8 = (tm + 2 * hl) // 8, tm // 8
    y, k0 = None, 0
    up = lambda j: jnp.dot(h_ref[...], wu_ref[j], preferred_element_type=F32).reshape(nt8, 8, 2 * FFN_CHUNK)
    ahead = [up(j) for j in range(FFN_LOOKAHEAD)]
    for j in range(FFN_NCHUNK):
        u3 = ahead.pop(0)
        if j + FFN_LOOKAHEAD < FFN_NCHUNK:
            ahead.append(up(j + FFN_LOOKAHEAD))
        w = cw_ref[j]
        cv = cb_ref[j]
        for kk in range(3):
            cv = cv + _shift_tiles(u3, kk - 1, 1, nm8) * w[kk:kk + 1, :]
        cv = cv.reshape(tm, 2 * FFN_CHUNK)
        act_ref[:, j * FFN_CHUNK:(j + 1) * FFN_CHUNK] = (_silu(cv[:, :FFN_CHUNK]) * cv[:, FFN_CHUNK:]).astype(BF16)
        if (j + 1) % FFN_DOWN_EVERY == 0 or j == FFN_NCHUNK - 1:
            k1 = (j + 1) * FFN_CHUNK
            part = jnp.dot(act_ref[:, k0:k1], wd_ref[k0:k1, :], preferred_element_type=F32)
            y = part if y is None else y + part
            k0 = k1
    out = xm_ref[...] + mod_ref[5:6, :] * y
    if final:
        out = _rms(out) * fg_ref[...]
    o_ref[...] = out


def _ffn(x, mod, layer, row, g, p, final_g=None):
    wu, cw, cb, wd = p
    nb, t, _ = x.shape
    tm = _tile(t, TOKEN_TILE)
    nt = t // tm
    hl = FFN_HALO
    r = tm // hl
    final = final_g is not None
    in_specs = [pl.BlockSpec((None, tm, D), lambda b, i: (b, i, 0)),
                pl.BlockSpec((None, hl, D), lambda b, i: (b, jnp.maximum(i * r - 1, 0), 0)),
                pl.BlockSpec((None, hl, D), lambda b, i: (b, jnp.minimum((i + 1) * r, t // hl - 1), 0)),
                _mod_spec(layer, row), _const_spec((1, D)),
                _const_spec((FFN_NCHUNK, D, 2 * FFN_CHUNK)), _const_spec((FFN_NCHUNK, 3, 2 * FFN_CHUNK)),
                _const_spec((FFN_NCHUNK, 1, 2 * FFN_CHUNK)), _const_spec((FFN_HIDDEN, D))]
    args = [x, x, x, mod, g, wu, cw, cb, wd]
    if final:
        in_specs.append(_const_spec((1, D)))
        args.append(final_g)
    return pl.pallas_call(
        functools.partial(_ffn_kernel, tm=tm, nt=nt, final=final),
        grid=(nb, nt),
        in_specs=in_specs,
        out_specs=pl.BlockSpec((None, tm, D), lambda b, i: (b, i, 0)),
        out_shape=jax.ShapeDtypeStruct((nb, t, D), F32),
        scratch_shapes=[pltpu.VMEM((tm + 2 * hl, D), BF16), pltpu.VMEM((tm, FFN_HIDDEN), BF16)],
        compiler_params=_params(("parallel", "parallel")),
    )(*args)


def _pair_cols(a, nchunk, width):
    lead = a.shape[:-1]
    a = a.reshape(lead + (2, nchunk, width))
    a = jnp.moveaxis(a, -2, 0)
    return a.reshape((nchunk,) + lead + (2 * width,))


def _ffn_weights(w_up, conv_w, conv_b, w_down):
    pair = lambda a: _pair_cols(a, FFN_NCHUNK, FFN_CHUNK)
    return (pair(w_up).astype(BF16), pair(conv_w), pair(conv_b[None, :]),
            w_down.astype(BF16))


def kernel(x, c, ctx, c_ctx, mod_w, mod_b, norm1_g, norm2_g, ffn_w_up, ffn_conv_w, ffn_conv_b, ffn_w_down,
           ssm_w_in, ssm_conv_w, ssm_conv_b, ssm_dt_bias, ssm_a_log, ssm_d, ssm_norm_g, ssm_w_out, attn_w_in,
           attn_lambda, attn_norm_g, attn_w_out, conf_w_pw1, conf_b_pw1, conf_dw_w, conf_dw_b, conf_ln_g,
           conf_ln_b, conf_w_pw2, conf_b_pw2, final_g):
    depth = mod_w.shape[0]
    nb = x.shape[0]
    mod = _mod_table(c, c_ctx, mod_w, mod_b)
    xc = ctx
    row = lambda a: a.reshape(1, -1)
    pad_dt = lambda a: jnp.pad(a.reshape(1, -1), ((0, 0), (0, DT_PAD - 2 * SSM_HEADS)))
    for i in range(depth):
        kind, j = i % N_MIXERS, i // N_MIXERS
        need_ctx = i < depth - 1
        g1 = row(norm1_g[i])
        if kind == 0:
            w_in = ssm_w_in[j]
            o_x, o_dt = D_INNER, D_INNER + SSM_CONV_CH
            wd = jnp.pad(w_in[:, o_dt:], ((0, 0), (0, DT_PAD - 2 * SSM_HEADS)))
            p = (w_in[:, :o_x].astype(BF16), w_in[:, o_x:o_dt].astype(BF16), wd.astype(BF16),
                 ssm_conv_w[j], ssm_conv_b[j][:, None, :], pad_dt(ssm_dt_bias[j]), pad_dt(ssm_a_log[j]),
                 jnp.repeat(ssm_d[j], SSM_HEAD_DIM, axis=-1)[:, None, :], row(ssm_norm_g[j]),
                 ssm_w_out[j].astype(BF16))
            x, xc = _mamba_layer(x, xc, mod, i, nb, g1, p, need_ctx)
        elif kind == 1:
            w_in = attn_w_in[j].astype(BF16)
            p = (w_in[:, :D], w_in[:, D:2 * D], w_in[:, 2 * D:], attn_lambda[j], row(attn_norm_g[j]),
                 attn_w_out[j].astype(BF16))
            x, xc = _attn_layer(x, xc, mod, i, nb, g1, p, need_ctx)
        else:
            p = (_pair_cols(conf_w_pw1[j], D // CONF_COLS, CONF_COLS).astype(BF16),
                 _pair_cols(row(conf_b_pw1[j]), D // CONF_COLS, CONF_COLS), conf_dw_w[j], row(conf_dw_b[j]),
                 row(conf_ln_g[j]), row(conf_ln_b[j]), conf_w_pw2[j].astype(BF16), row(conf_b_pw2[j]))
            x = _conformer(x, mod, i, None, g1, p)
            if need_ctx:
                xc = _conformer(xc, mod, i, nb, g1, p)
        fp = _ffn_weights(ffn_w_up[i], ffn_conv_w[i], ffn_conv_b[i], ffn_w_down[i])
        g2 = row(norm2_g[i])
        x = _ffn(x, mod, i, None, g2, fp, final_g=row(final_g) if i == depth - 1 else None)
        if need_ctx:
            xc = _ffn(xc, mod, i, nb, g2, fp)
    return x
```

```python
import functools
import math

import jax
import jax.numpy as jnp
from jax import lax
from jax.experimental import pallas as pl
from jax.experimental.pallas import tpu as pltpu

F32 = jnp.float32
BF16 = jnp.bfloat16
HIGHEST = lax.Precision.HIGHEST

D = 1024
N_MOD = 6
N_MIXERS = 3
EPS = 1e-6
GRID_W = 64
ROPE_THETA = 10000.0

SSM_HEAD_DIM = 64
SSM_HEADS = 32
SSM_GROUPS = 8
SSM_HPG = SSM_HEADS // SSM_GROUPS
SSM_STATE = 128
SSM_CONV = 4
SSM_CHUNK = 128
D_INNER = SSM_HEADS * SSM_HEAD_DIM
SSM_GW = SSM_HPG * SSM_HEAD_DIM
SSM_BC = SSM_GROUPS * SSM_STATE
SSM_CONV_CH = D_INNER + 2 * SSM_BC
DT_PAD = 128
SSD_HALO = 16

DA_HEAD_DIM = 64
DA_HEADS = D // (2 * DA_HEAD_DIM)
DA_V = 2 * DA_HEAD_DIM

CONF_KERNEL = 31
CONF_HALO = 16
CONF_COLS = 256

FFN_HIDDEN = 2816
FFN_CHUNK = 256
FFN_NCHUNK = FFN_HIDDEN // FFN_CHUNK
FFN_HALO = 8
FFN_LOOKAHEAD = 3
FFN_DOWN_EVERY = 4

VMEM_LIMIT = 56 * 1024 * 1024
TOKEN_TILE = 512
RESID_ROW_BLOCKS = 2
ATTN_Q_TILE = 256
ATTN_HEADS_PER_STEP = 4
QKV_COLS = 512


def _sigmoid(x):
    return 1.0 / (1.0 + jnp.exp2(x * -math.log2(math.e)))


def _silu(x):
    return x * _sigmoid(x)


def _softplus(x):
    return jnp.maximum(x, 0.0) + jnp.log(1.0 + jnp.exp(-jnp.abs(x)))


def _rms(x):
    return x * lax.rsqrt(jnp.mean(x * x, axis=-1, keepdims=True) + EPS)


def _shift_tiles(x3, s, first, count):
    if s == 0:
        return x3[first:first + count]
    sub = lax.broadcasted_iota(jnp.int32, (count,) + x3.shape[1:], 1)
    rk = pltpu.roll(x3, (-s) % 8, 1)
    if s < 0:
        return jnp.where(sub >= -s, rk[first:first + count], rk[first - 1:first - 1 + count])
    return jnp.where(sub < 8 - s, rk[first:first + count], rk[first + 1:first + 1 + count])


def _dot01(m01, x, *, left):
    out, rest = None, x
    for _ in range(3):
        piece = rest.astype(BF16)
        part = jnp.dot(m01, piece, preferred_element_type=F32) if left else \
            jnp.dot(piece, m01, preferred_element_type=F32)
        out = part if out is None else out + part
        rest = rest - piece.astype(F32)
    return out


def _params(sem, vmem=VMEM_LIMIT):
    return pltpu.CompilerParams(dimension_semantics=sem, vmem_limit_bytes=vmem)


def _mod_spec(layer, row):
    if row is None:
        return pl.BlockSpec((None, None, N_MOD, D), lambda b, *_: (layer, b, 0, 0))
    return pl.BlockSpec((None, None, N_MOD, D), lambda b, *_: (layer, row, 0, 0))


def _const_spec(shape):
    nd = len(shape)
    return pl.BlockSpec(shape, lambda *_: (0,) * nd)


def _tile(n, pref):
    return pref if n % pref == 0 else n


def _mod_kernel(s_ref, w_ref, b_ref, o_ref):
    s = _silu(s_ref[...])
    o_ref[...] = jnp.dot(s, w_ref[...], preferred_element_type=F32, precision=HIGHEST) + b_ref[...]


def _mod_table(c, c_ctx, mod_w, mod_b):
    depth = mod_w.shape[0]
    nb = c.shape[0]
    rows = -(-(nb + 1) // 8) * 8
    s = jnp.zeros((rows, D), F32).at[:nb].set(c).at[nb].set(c_ctx)
    tn = 1536
    out = pl.pallas_call(
        _mod_kernel,
        grid=(depth, N_MOD * D // tn),
        in_specs=[pl.BlockSpec((rows, D), lambda i, n: (0, 0)),
                  pl.BlockSpec((None, D, tn), lambda i, n: (i, 0, n)),
                  pl.BlockSpec((None, 1, tn), lambda i, n: (i, 0, n))],
        out_specs=pl.BlockSpec((None, rows, tn), lambda i, n: (i, 0, n)),
        out_shape=jax.ShapeDtypeStruct((depth, rows, N_MOD * D), F32),
        compiler_params=_params(("arbitrary", "arbitrary")),
    )(s, mod_w, mod_b.reshape(depth, 1, N_MOD * D))
    return out.reshape(depth, rows, N_MOD, D)


def _mamba_in_kernel(x_ref, mod_ref, g_ref, wz_ref, wx_ref, wd_ref, z_ref, xbc_ref, dt_ref):
    gs = g_ref[...] * (1.0 + mod_ref[1:2, :])
    h = (_rms(x_ref[...]) * gs + mod_ref[0:1, :]).astype(BF16)
    for n in range(0, D_INNER, 1024):
        z_ref[:, n:n + 1024] = jnp.dot(h, wz_ref[:, n:n + 1024], preferred_element_type=F32).astype(z_ref.dtype)
    for n in range(0, SSM_CONV_CH, 1024):
        xbc_ref[:, n:n + 1024] = jnp.dot(h, wx_ref[:, n:n + 1024],
                                         preferred_element_type=F32).astype(xbc_ref.dtype)
    dt_ref[...] = jnp.dot(h, wd_ref[...], preferred_element_type=F32)


def _mamba_in(x, mod, layer, row, g, wz, wx, wd):
    nb, t, _ = x.shape
    tm = _tile(t, TOKEN_TILE)
    tok = lambda w: pl.BlockSpec((None, tm, w), lambda b, i: (b, i, 0))
    return pl.pallas_call(
        _mamba_in_kernel,
        grid=(nb, t // tm),
        in_specs=[tok(D), _mod_spec(layer, row), _const_spec((1, D)),
                  _const_spec((D, D_INNER)), _const_spec((D, SSM_CONV_CH)), _const_spec((D, DT_PAD))],
        out_specs=[tok(D_INNER), tok(SSM_CONV_CH), tok(DT_PAD)],
        out_shape=[jax.ShapeDtypeStruct((nb, t, D_INNER), BF16),
                   jax.ShapeDtypeStruct((nb, t, SSM_CONV_CH), BF16),
                   jax.ShapeDtypeStruct((nb, t, DT_PAD), F32)],
        compiler_params=_params(("parallel", "parallel")),
    )(x, mod, g, wz, wx, wd)


def _ssd_kernel(*refs, rev, nc, write_y, combine):
    it = iter(refs)
    u_ref, xh_ref, cw_ref, cbias_ref = (next(it) for _ in range(4))
    dt_ref, dtb_ref, alog_ref, dsk_ref, e_ref, h0_ref = (next(it) for _ in range(6))
    if combine:
        yf_ref = next(it)
    y_ref = next(it) if write_y else None
    hfin_ref = next(it)
    ht_ref = next(it)

    k = pl.program_id(1)
    ch = SSM_CHUNK
    off = SSM_HEADS if rev else 0

    @pl.when(k == 0)
    def _():
        ht_ref[...] = h0_ref[...]

    c = (nc - 1 - k) if rev else k
    side_on = (c < nc - 1) if rev else (c > 0)

    def load(c0, w):
        main = u_ref[:, c0:c0 + w].astype(F32)
        side = jnp.where(side_on, xh_ref[:, c0:c0 + w].astype(F32), 0.0)
        buf = jnp.concatenate([main, side] if rev else [side, main], axis=0)
        x3 = buf.reshape((ch + SSD_HALO) // 8, 8, w)
        first = 0 if rev else SSD_HALO // 8
        acc = cbias_ref[:, c0:c0 + w]
        for kk in range(SSM_CONV):
            s = (SSM_CONV - 1 - kk) * (1 if rev else -1)
            acc = acc + _shift_tiles(x3, s, first, ch // 8) * cw_ref[kk:kk + 1, c0:c0 + w]
        return _silu(acc).reshape(ch, w)

    dtv = _softplus(dt_ref[...] + dtb_ref[...])
    la = dtv * (-jnp.exp(alog_ref[...]))
    ri = lax.broadcasted_iota(jnp.int32, (ch, ch), 0)
    ci = lax.broadcasted_iota(jnp.int32, (ch, ch), 1)
    keep = (ci >= ri) if rev else (ci <= ri)
    causal = jnp.where(keep, 0.0, -jnp.inf)
    acum = _dot01(jnp.where(keep, 1.0, 0.0).astype(BF16), la, left=True)
    last = 0 if rev else ch - 1
    total = acum[last:last + 1, :]
    wcol = jnp.exp(total - acum) * dtv
    offv = jnp.exp(acum)
    acum2 = acum * math.log2(math.e)
    acum2_t = acum2.T
    dt_t = dtv.T

    lane = lax.broadcasted_iota(jnp.int32, (ch, SSM_GW), 1)

    wexp = jnp.dot(wcol.astype(BF16), e_ref[...], preferred_element_type=F32)
    sexp = jnp.dot(offv.astype(BF16), e_ref[...], preferred_element_type=F32)
    cdec = _dot01(e_ref[...], jnp.broadcast_to(offv[last:last + 1, :], (8, DT_PAD)), left=False)[0:1, :]

    for g in range(SSM_GROUPS):
        gc = slice(g * SSM_GW, (g + 1) * SSM_GW)
        xg = load(g * SSM_GW, SSM_GW)
        xg16 = xg.astype(BF16)
        bg16 = load(D_INNER + g * SSM_STATE, SSM_STATE).astype(BF16)
        xw = (xg * wexp[:, gc]).astype(BF16)
        st_t = lax.dot_general(bg16, xw, (((0,), (0,)), ((), ())), preferred_element_type=F32)
        sg = sexp[:, gc]
        ht_old = ht_ref[g]
        if write_y:
            cg16 = load(D_INNER + SSM_BC + g * SSM_STATE, SSM_STATE).astype(BF16)
            cbm = lax.dot_general(cg16, bg16, (((1,), (1,)), ((), ())), preferred_element_type=F32)
            yoff = jnp.dot(cg16, ht_old.astype(BF16), preferred_element_type=F32)
            ms, xs = [], []
            for j in range(SSM_HPG):
                col = off + SSM_HPG * g + j
                seg = acum2[:, col:col + 1] - acum2_t[col:col + 1, :]
                lmat = jnp.exp2(seg + causal)
                ms.append((cbm * lmat * dt_t[col:col + 1, :]).astype(BF16))
                in_head = (lane >= j * SSM_HEAD_DIM) & (lane < (j + 1) * SSM_HEAD_DIM)
                xs.append(jnp.where(in_head, xg16, jnp.zeros_like(xg16)))
            ydiag = jnp.dot(jnp.concatenate(ms, axis=1), jnp.concatenate(xs, axis=0),
                            preferred_element_type=F32)
            y = ydiag + yoff * sg + dsk_ref[:, g * SSM_GW:(g + 1) * SSM_GW] * xg
            cols = slice(g * SSM_GW, (g + 1) * SSM_GW)
            if combine:
                y = y + yf_ref[:, cols].astype(F32)
            y_ref[:, cols] = y.astype(y_ref.dtype)
        ht_ref[g] = ht_old * cdec[:, gc] + st_t

    @pl.when(k == nc - 1)
    def _():
        hfin_ref[...] = ht_ref[...]


def _ssd(u, dt, h0, dt_bias, a_log, d_skip, taps, *, rev, write_y, combine=None, y_dtype=F32):
    nb, t, _ = u.shape
    nc = t // SSM_CHUNK
    hb = SSM_CHUNK // SSD_HALO
    if rev:
        cidx = lambda k: nc - 1 - k
        hidx = lambda k: jnp.minimum((nc - k) * hb, t // SSD_HALO - 1)
    else:
        cidx = lambda k: k
        hidx = lambda k: jnp.maximum(k * hb - 1, 0)
    chunk = lambda w: pl.BlockSpec((None, SSM_CHUNK, w), lambda b, k: (b, cidx(k), 0))
    state = pl.BlockSpec((None, SSM_GROUPS, SSM_STATE, SSM_GW), lambda b, k: (b, 0, 0, 0))
    in_specs = [chunk(SSM_CONV_CH), pl.BlockSpec((None, SSD_HALO, SSM_CONV_CH), lambda b, k: (b, hidx(k), 0)),
                _const_spec((SSM_CONV, SSM_CONV_CH)), _const_spec((1, SSM_CONV_CH))]
    args = [u, u, taps[0], taps[1]]
    scratch = [pltpu.VMEM((SSM_GROUPS, SSM_STATE, SSM_GW), F32)]
    off = SSM_HEADS if rev else 0
    spread = jnp.repeat(jnp.eye(DT_PAD, dtype=BF16)[:, off:off + SSM_HEADS], SSM_HEAD_DIM, axis=1)
    in_specs += [chunk(DT_PAD), _const_spec((1, DT_PAD)), _const_spec((1, DT_PAD)), _const_spec((1, D_INNER)),
                 _const_spec((DT_PAD, D_INNER)), state]
    args += [dt, dt_bias, a_log, d_skip, spread, h0]
    out_specs, out_shape = [], []
    if combine is not None:
        in_specs.append(chunk(D_INNER))
        args.append(combine)
    if write_y:
        out_specs.append(chunk(D_INNER))
        out_shape.append(jax.ShapeDtypeStruct((nb, t, D_INNER), y_dtype))
    out_specs.append(state)
    out_shape.append(jax.ShapeDtypeStruct((nb, SSM_GROUPS, SSM_STATE, SSM_GW), F32))
    outs = pl.pallas_call(
        functools.partial(_ssd_kernel, rev=rev, nc=nc, write_y=write_y, combine=combine is not None),
        grid=(nb, nc),
        in_specs=in_specs, out_specs=out_specs, out_shape=out_shape,
        scratch_shapes=scratch,
        compiler_params=_params(("parallel", "arbitrary")),
    )(*args)
    return outs if write_y else (None, outs[0])


def _resid_kernel(*refs, gate_row, ssm_gate):
    if not ssm_gate:
        a_ref, w_ref, b_ref, x_ref, mod_ref, o_ref = refs
        y = jnp.dot(a_ref[...], w_ref[...], preferred_element_type=F32) + b_ref[...]
        o_ref[...] = x_ref[...] + mod_ref[gate_row:gate_row + 1, :] * y
        return
    a_ref, z_ref, ng_ref, w_ref, b_ref, x_ref, mod_ref, o_ref = refs
    tm = a_ref.shape[0]
    rb = tm // RESID_ROW_BLOCKS
    for r0 in range(0, tm, rb):
        parts = []
        for g in range(SSM_GROUPS):
            cols = slice(g * SSM_GW, (g + 1) * SSM_GW)
            yg = a_ref[r0:r0 + rb, cols].astype(F32) * _silu(z_ref[r0:r0 + rb, cols].astype(F32))
            parts.append((_rms(yg) * ng_ref[:, cols]).astype(BF16))
        y = jnp.dot(jnp.concatenate(parts, axis=1), w_ref[...], preferred_element_type=F32) + b_ref[...]
        o_ref[r0:r0 + rb, :] = x_ref[r0:r0 + rb, :] + mod_ref[gate_row:gate_row + 1, :] * y


def _resid_matmul(a, w, bias, x, mod, layer, row, gate_row, ssm_gate=None):
    nb, t, kdim = a.shape
    tm = _tile(t, TOKEN_TILE)
    tok = lambda wd: pl.BlockSpec((None, tm, wd), lambda b, i: (b, i, 0))
    in_specs, args = [tok(kdim)], [a]
    if ssm_gate is not None:
        in_specs += [tok(kdim), _const_spec((1, kdim))]
        args += list(ssm_gate)
    in_specs += [_const_spec((kdim, D)), _const_spec((1, D)), tok(D), _mod_spec(layer, row)]
    args += [w, bias, x, mod]
    return pl.pallas_call(
        functools.partial(_resid_kernel, gate_row=gate_row, ssm_gate=ssm_gate is not None),
        grid=(nb, t // tm),
        in_specs=in_specs,
        out_specs=tok(D),
        out_shape=jax.ShapeDtypeStruct((nb, t, D), F32),
        compiler_params=_params(("parallel", "parallel")),
    )(*args)


def _mamba_layer(x, xc, mod, layer, nb, g1, p, need_ctx):
    wz, wx, wd, cw, cbias, dtb, alog, dsk, ng, wout = p
    z_l, xbc_l, dt_l = _mamba_in(x, mod, layer, None, g1, wz, wx, wd)
    z_c, xbc_c, dt_c = _mamba_in(xc, mod, layer, nb, g1, wz, wx, wd)
    h0 = jnp.zeros((x.shape[0], SSM_GROUPS, SSM_STATE, SSM_GW), F32)
    dirp = lambda d: (dtb, alog, dsk[d])
    taps = lambda d: (cw[d], cbias[d])
    yf_c, hf = _ssd(xbc_c, dt_c, h0, *dirp(0), rev=False, write_y=need_ctx, taps=taps(0))
    yf_l, _ = _ssd(xbc_l, dt_l, hf, *dirp(0), rev=False, write_y=True, taps=taps(0))
    y_c, hb = _ssd(xbc_c, dt_c, h0, *dirp(1), rev=True, write_y=need_ctx, taps=taps(1),
                   combine=yf_c if need_ctx else None, y_dtype=BF16)
    y_l, _ = _ssd(xbc_l, dt_l, hb, *dirp(1), rev=True, write_y=True, taps=taps(1), combine=yf_l, y_dtype=BF16)
    zero_b = jnp.zeros((1, D), F32)
    x = _resid_matmul(y_l, wout, zero_b, x, mod, layer, None, 2, ssm_gate=(z_l, ng))
    if need_ctx:
        xc = _resid_matmul(y_c, wout, zero_b, xc, mod, layer, nb, 2, ssm_gate=(z_c, ng))
    return x, xc


def _qkv_kernel(*refs, rope):
    if rope:
        x_ref, mod_ref, g_ref, wq_ref, wk_ref, wv_ref, cos_ref, sin_ref, q_ref, k_ref, v_ref = refs
    else:
        x_ref, mod_ref, g_ref, wq_ref, wk_ref, wv_ref, q_ref, k_ref, v_ref = refs
    gs = g_ref[...] * (1.0 + mod_ref[1:2, :])
    h = (_rms(x_ref[...]) * gs + mod_ref[0:1, :]).astype(BF16)
    tm = h.shape[0]
    if rope:
        cos, sin = cos_ref[...], sin_ref[...]
        lane = lax.broadcasted_iota(jnp.int32, (tm, 128), 1)
        first = (lane % 32) < 16

    def rot(a_ref, w_ref, scale):
        for j2 in range(D // QKV_COLS):
            wide = jnp.dot(h, w_ref[:, j2 * QKV_COLS:(j2 + 1) * QKV_COLS], preferred_element_type=F32)
            for j in range(QKV_COLS // 128):
                blk = wide[:, j * 128:(j + 1) * 128]
                if scale != 1.0:
                    blk = blk * scale
                if rope:
                    partner = jnp.where(first, pltpu.roll(blk, 112, 1), pltpu.roll(blk, 16, 1))
                    blk = blk * cos + partner * sin
                c0 = j2 * QKV_COLS + j * 128
                a_ref[:, c0:c0 + 128] = blk.astype(a_ref.dtype)

    rot(q_ref, wq_ref, DA_HEAD_DIM ** -0.5 * math.log2(math.e))
    rot(k_ref, wk_ref, 1.0)
    v_ref[...] = jnp.dot(h, wv_ref[...], preferred_element_type=F32).astype(v_ref.dtype)


def _rope_tables(s):
    lane = jnp.arange(128)
    pos = jnp.arange(s)
    coord = jnp.where(((lane % 64) // 32 == 0)[None, :], (pos // GRID_W)[:, None], (pos % GRID_W)[:, None])
    n_freq = DA_HEAD_DIM // 4
    inv = ROPE_THETA ** (-(lane % n_freq).astype(F32) / n_freq)
    ang = coord.astype(F32) * inv[None, :]
    sign = jnp.where((lane % 32) < 16, -1.0, 1.0)
    return jnp.cos(ang), jnp.sin(ang) * sign[None, :]


def _qkv(x, mod, layer, row, g, wq, wk, wv, tables):
    nb, t, _ = x.shape
    tm = _tile(t, TOKEN_TILE)
    tok = lambda: pl.BlockSpec((None, tm, D), lambda b, i: (b, i, 0))
    in_specs = [tok(), _mod_spec(layer, row), _const_spec((1, D))] + [_const_spec((D, D))] * 3
    args = [x, mod, g, wq, wk, wv]
    if tables is not None:
        in_specs += [pl.BlockSpec((tm, 128), lambda b, i: (i, 0))] * 2
        args += list(tables)
    return pl.pallas_call(
        functools.partial(_qkv_kernel, rope=tables is not None),
        grid=(nb, t // tm),
        in_specs=in_specs, out_specs=[tok()] * 3,
        out_shape=[jax.ShapeDtypeStruct((nb, t, D), BF16)] * 3,
        compiler_params=_params(("parallel", "parallel")),
    )(*args)


def _attn_kernel(*refs, nseg, lam_init):
    lam_ref, ng_ref, q_ref = refs[:3]
    kv = refs[3:3 + 2 * nseg]
    o_ref = refs[3 + 2 * nseg]
    tq = q_ref.shape[0]
    nh = q_ref.shape[1] // DA_V
    lane = lax.broadcasted_iota(jnp.int32, (tq, DA_V), 1)
    lp = lam_ref[...]
    lam = (jnp.exp(jnp.sum(lp[0:1] * lp[1:2], axis=-1, keepdims=True))
           - jnp.exp(jnp.sum(lp[2:3] * lp[3:4], axis=-1, keepdims=True)) + lam_init)

    def head_scores(hh):
        hc = slice(hh * DA_V, (hh + 1) * DA_V)
        q = q_ref[:, hc]
        zero = jnp.zeros_like(q)
        qs = [jnp.where((lane >= e * DA_HEAD_DIM) & (lane < (e + 1) * DA_HEAD_DIM), q, zero) for e in range(2)]
        return [[lax.dot_general(qe, kv[2 * i][:, hc], (((1,), (1,)), ((), ())), preferred_element_type=F32)
                 for i in range(nseg)] for qe in qs]

    def head_out(hh, scores):
        hc = slice(hh * DA_V, (hh + 1) * DA_V)
        vs = [jnp.concatenate([kv[2 * i + 1][:, hc], jnp.ones((kv[2 * i + 1].shape[0], DA_V), BF16)], axis=1)
              for i in range(nseg)]
        halves = []
        for e in range(2):
            m = scores[e][0].max(axis=-1, keepdims=True)
            for s in scores[e][1:]:
                m = jnp.maximum(m, s.max(axis=-1, keepdims=True))
            acc = jnp.zeros((tq, 2 * DA_V), F32)
            for i in range(nseg):
                p = jnp.exp2(scores[e][i] - m)
                acc = acc + jnp.dot(p.astype(BF16), vs[i], preferred_element_type=F32)
            halves.append(acc[:, :DA_V] / acc[:, DA_V:DA_V + 1])
        o = halves[0] - lam * halves[1]
        o_ref[:, hc] = (_rms(o) * ng_ref[...] * (1.0 - lam_init)).astype(o_ref.dtype)

    all_scores = [head_scores(hh) for hh in range(nh)]
    for hh in range(nh):
        head_out(hh, all_scores[hh])


def _attention(q, segs, lam_p, ng, lam_init):
    nb, t, _ = q.shape
    tq = _tile(t, ATTN_Q_TILE)
    hw = ATTN_HEADS_PER_STEP * DA_V
    in_specs = [_const_spec((4, DA_HEAD_DIM)), _const_spec((1, DA_V)),
                pl.BlockSpec((None, tq, hw), lambda b, h, i: (b, i, h))]
    args = [lam_p, ng, q]
    for k, v in segs:
        tk = k.shape[1]
        in_specs += [pl.BlockSpec((None, tk, hw), lambda b, h, i: (b, 0, h))] * 2
        args += [k, v]
    return pl.pallas_call(
        functools.partial(_attn_kernel, nseg=len(segs), lam_init=lam_init),
        grid=(nb, DA_HEADS // ATTN_HEADS_PER_STEP, t // tq),
        in_specs=in_specs,
        out_specs=pl.BlockSpec((None, tq, hw), lambda b, h, i: (b, i, h)),
        out_shape=jax.ShapeDtypeStruct((nb, t, D), BF16),
        compiler_params=_params(("parallel", "parallel", "arbitrary")),
    )(*args)


def _attn_layer(x, xc, mod, layer, nb, g1, p, need_ctx):
    wq, wk, wv, lam_p, ng, wout = p
    lam_init = 0.8 - 0.6 * math.exp(-0.3 * layer)
    q_l, k_l, v_l = _qkv(x, mod, layer, None, g1, wq, wk, wv, _rope_tables(x.shape[1]))
    q_c, k_c, v_c = _qkv(xc, mod, layer, nb, g1, wq, wk, wv, None)
    zero_b = jnp.zeros((1, D), F32)
    o_l = _attention(q_l, [(k_c, v_c), (k_l, v_l)], lam_p, ng, lam_init)
    x = _resid_matmul(o_l, wout, zero_b, x, mod, layer, None, 2)
    if need_ctx:
        o_c = _attention(q_c, [(k_c, v_c)], lam_p, ng, lam_init)
        xc = _resid_matmul(o_c, wout, zero_b, xc, mod, layer, nb, 2)
    return x, xc


def _conf_kernel(xm_ref, xp_ref, xn_ref, mod_ref, g_ref, w1_ref, b1_ref, dw_ref, db_ref, lg_ref, lb_ref,
                 w2_ref, b2_ref, o_ref, h_ref, u_ref, c_ref, sh_ref, *, tm, nt):
    t = pl.program_id(1)
    hl = CONF_HALO
    gs = g_ref[...] * (1.0 + mod_ref[1:2, :])
    shift = mod_ref[0:1, :]
    mod = lambda x: _rms(x) * gs + shift
    h_ref[...] = jnp.concatenate([mod(xp_ref[...]), mod(xm_ref[...]), mod(xn_ref[...])], axis=0).astype(BF16)
    rowid = lax.broadcasted_iota(jnp.int32, (tm + 2 * hl, 1), 0)
    valid = ((rowid >= hl) | (t > 0)) & ((rowid < hl + tm) | (t < nt - 1))

    rb, cw = 64, CONF_COLS
    pad = CONF_KERNEL // 2
    nt8 = (tm + 2 * hl) // 8
    glu_in = lambda cb: jnp.dot(h_ref[...], w1_ref[cb], preferred_element_type=F32) + b1_ref[cb]
    a_next = glu_in(0)
    for c0 in range(0, D, cw):
        a = a_next
        if c0 + cw < D:
            a_next = glu_in(c0 // cw + 1)
        u = jnp.where(valid, a[:, :cw] * _sigmoid(a[:, cw:]), 0.0)
        u_ref[:, c0:c0 + cw] = u
        u3 = u.reshape(nt8, 8, cw)
        for r in range(1, 8):
            sh_ref[r - 1, 8:, c0:c0 + cw] = _shift_tiles(u3, -r, 1, nt8 - 1).reshape(8 * (nt8 - 1), cw)
        for r0 in range(0, tm, rb):
            acc = jnp.broadcast_to(db_ref[:, c0:c0 + cw], (rb // 8, 8, cw))
            for kk in range(CONF_KERNEL):
                d = kk - pad
                r = (-d) % 8
                s = hl + d + r + r0
                src = u_ref if r == 0 else sh_ref.at[r - 1]
                acc = acc + src[s:s + rb, c0:c0 + cw].reshape(rb // 8, 8, cw) * dw_ref[kk, :, c0:c0 + cw]
            c_ref[r0:r0 + rb, c0:c0 + cw] = acc.reshape(rb, cw)

    cv = c_ref[...]
    mu = jnp.mean(cv, axis=-1, keepdims=True)
    xc = cv - mu
    ln = xc * lax.rsqrt(jnp.mean(xc * xc, axis=-1, keepdims=True) + EPS) * lg_ref[...] + lb_ref[...]
    y = jnp.dot(_silu(ln).astype(BF16), w2_ref[...], preferred_element_type=F32) + b2_ref[...]
    o_ref[...] = xm_ref[...] + mod_ref[2:3, :] * y


def _conformer(x, mod, layer, row, g, p):
    w1, b1, dw, db, lg, lb, w2, b2 = p
    nb, t, _ = x.shape
    tm = _tile(t, 256)
    nt = t // tm
    hl = CONF_HALO
    r = tm // hl
    return pl.pallas_call(
        functools.partial(_conf_kernel, tm=tm, nt=nt),
        grid=(nb, nt),
        in_specs=[pl.BlockSpec((None, tm, D), lambda b, i: (b, i, 0)),
                  pl.BlockSpec((None, hl, D), lambda b, i: (b, jnp.maximum(i * r - 1, 0), 0)),
                  pl.BlockSpec((None, hl, D), lambda b, i: (b, jnp.minimum((i + 1) * r, t // hl - 1), 0)),
                  _mod_spec(layer, row), _const_spec((1, D)),
                  _const_spec((D // CONF_COLS, D, 2 * CONF_COLS)), _const_spec((D // CONF_COLS, 1, 2 * CONF_COLS)),
                  _const_spec((CONF_KERNEL, 8, D)), _const_spec((1, D)), _const_spec((1, D)), _const_spec((1, D)),
                  _const_spec((D, D)), _const_spec((1, D))],
        out_specs=pl.BlockSpec((None, tm, D), lambda b, i: (b, i, 0)),
        out_shape=jax.ShapeDtypeStruct((nb, t, D), F32),
        scratch_shapes=[pltpu.VMEM((tm + 2 * hl, D), BF16), pltpu.VMEM((tm + 2 * hl, D), F32),
                        pltpu.VMEM((tm, D), F32), pltpu.VMEM((7, tm + 2 * hl, D), F32)],
        compiler_params=_params(("parallel", "parallel")),
    )(x, x, x, mod, g, w1, b1, dw, db, lg, lb, w2, b2)


def _ffn_kernel(*refs, tm, nt, final):
    if final:
        (xm_ref, xp_ref, xn_ref, mod_ref, g_ref, wu_ref, cw_ref, cb_ref, wd_ref, fg_ref,
         o_ref, h_ref, act_ref) = refs
    else:
        (xm_ref, xp_ref, xn_ref, mod_ref, g_ref, wu_ref, cw_ref, cb_ref, wd_ref,
         o_ref, h_ref, act_ref) = refs
    t = pl.program_id(1)
    hl = FFN_HALO
    gs = g_ref[...] * (1.0 + mod_ref[4:5, :])
    shift = mod_ref[3:4, :]
    mod = lambda x: _rms(x) * gs + shift
    hp = jnp.where(t > 0, mod(xp_ref[...]), 0.0)
    hn = jnp.where(t < nt - 1, mod(xn_ref[...]), 0.0)
    h_ref[...] = jnp.concatenate([hp, mod(xm_ref[...]), hn], axis=0).astype(BF16)

    nt8, nm8 = (tm + 2 * hl) // 8, tm // 8
    y, k0 = None, 0
    up = lambda j: jnp.dot(h_ref[...], wu_ref[j], preferred_element_type=F32).reshape(nt8, 8, 2 * FFN_CHUNK)
    ahead = [up(j) for j in range(FFN_LOOKAHEAD)]
    for j in range(FFN_NCHUNK):
        u3 = ahead.pop(0)
        if j + FFN_LOOKAHEAD < FFN_NCHUNK:
            ahead.append(up(j + FFN_LOOKAHEAD))
        w = cw_ref[j]
        cv = cb_ref[j]
        for kk in range(3):
            cv = cv + _shift_tiles(u3, kk - 1, 1, nm8) * w[kk:kk + 1, :]
        cv = cv.reshape(tm, 2 * FFN_CHUNK)
        act_ref[:, j * FFN_CHUNK:(j + 1) * FFN_CHUNK] = (_silu(cv[:, :FFN_CHUNK]) * cv[:, FFN_CHUNK:]).astype(BF16)
        if (j + 1) % FFN_DOWN_EVERY == 0 or j == FFN_NCHUNK - 1:
            k1 = (j + 1) * FFN_CHUNK
            part = jnp.dot(act_ref[:, k0:k1], wd_ref[k0:k1, :], preferred_element_type=F32)
            y = part if y is None else y + part
            k0 = k1
    out = xm_ref[...] + mod_ref[5:6, :] * y
    if final:
        out = _rms(out) * fg_ref[...]
    o_ref[...] = out


def _ffn(x, mod, layer, row, g, p, final_g=None):
    wu, cw, cb, wd = p
    nb, t, _ = x.shape
    tm = _tile(t, TOKEN_TILE)
    nt = t // tm
    hl = FFN_HALO
    r = tm // hl
    final = final_g is not None
    in_specs = [pl.BlockSpec((None, tm, D), lambda b, i: (b, i, 0)),
                pl.BlockSpec((None, hl, D), lambda b, i: (b, jnp.maximum(i * r - 1, 0), 0)),
                pl.BlockSpec((None, hl, D), lambda b, i: (b, jnp.minimum((i + 1) * r, t // hl - 1), 0)),
                _mod_spec(layer, row), _const_spec((1, D)),
                _const_spec((FFN_NCHUNK, D, 2 * FFN_CHUNK)), _const_spec((FFN_NCHUNK, 3, 2 * FFN_CHUNK)),
                _const_spec((FFN_NCHUNK, 1, 2 * FFN_CHUNK)), _const_spec((FFN_HIDDEN, D))]
    args = [x, x, x, mod, g, wu, cw, cb, wd]
    if final:
        in_specs.append(_const_spec((1, D)))
        args.append(final_g)
    return pl.pallas_call(
        functools.partial(_ffn_kernel, tm=tm, nt=nt, final=final),
        grid=(nb, nt),
        in_specs=in_specs,
        out_specs=pl.BlockSpec((None, tm, D), lambda b, i: (b, i, 0)),
        out_shape=jax.ShapeDtypeStruct((nb, t, D), F32),
        scratch_shapes=[pltpu.VMEM((tm + 2 * hl, D), BF16), pltpu.VMEM((tm, FFN_HIDDEN), BF16)],
        compiler_params=_params(("parallel", "parallel")),
    )(*args)


def _pair_cols(a, nchunk, width):
    lead = a.shape[:-1]
    a = a.reshape(lead + (2, nchunk, width))
    a = jnp.moveaxis(a, -2, 0)
    return a.reshape((nchunk,) + lead + (2 * width,))


def _ffn_weights(w_up, conv_w, conv_b, w_down):
    pair = lambda a: _pair_cols(a, FFN_NCHUNK, FFN_CHUNK)
    return (pair(w_up).astype(BF16), pair(conv_w), pair(conv_b[None, :]),
            w_down.astype(BF16))


def kernel(x, c, ctx, c_ctx, mod_w, mod_b, norm1_g, norm2_g, ffn_w_up, ffn_conv_w, ffn_conv_b, ffn_w_down,
           ssm_w_in, ssm_conv_w, ssm_conv_b, ssm_dt_bias, ssm_a_log, ssm_d, ssm_norm_g, ssm_w_out, attn_w_in,
           attn_lambda, attn_norm_g, attn_w_out, conf_w_pw1, conf_b_pw1, conf_dw_w, conf_dw_b, conf_ln_g,
           conf_ln_b, conf_w_pw2, conf_b_pw2, final_g):
    depth = mod_w.shape[0]
    nb = x.shape[0]
    mod = _mod_table(c, c_ctx, mod_w, mod_b)
    xc = ctx
    row = lambda a: a.reshape(1, -1)
    pad_dt = lambda a: jnp.pad(a.reshape(1, -1), ((0, 0), (0, DT_PAD - 2 * SSM_HEADS)))
    for i in range(depth):
        kind, j = i % N_MIXERS, i // N_MIXERS
        need_ctx = i < depth - 1
        g1 = row(norm1_g[i])
        if kind == 0:
            w_in = ssm_w_in[j]
            o_x, o_dt = D_INNER, D_INNER + SSM_CONV_CH
            wd = jnp.pad(w_in[:, o_dt:], ((0, 0), (0, DT_PAD - 2 * SSM_HEADS)))
            p = (w_in[:, :o_x].astype(BF16), w_in[:, o_x:o_dt].astype(BF16), wd.astype(BF16),
                 ssm_conv_w[j], ssm_conv_b[j][:, None, :], pad_dt(ssm_dt_bias[j]), pad_dt(ssm_a_log[j]),
                 jnp.repeat(ssm_d[j], SSM_HEAD_DIM, axis=-1)[:, None, :], row(ssm_norm_g[j]),
                 ssm_w_out[j].astype(BF16))
            x, xc = _mamba_layer(x, xc, mod, i, nb, g1, p, need_ctx)
        elif kind == 1:
            w_in = attn_w_in[j].astype(BF16)
            p = (w_in[:, :D], w_in[:, D:2 * D], w_in[:, 2 * D:], attn_lambda[j], row(attn_norm_g[j]),
                 attn_w_out[j].astype(BF16))
            x, xc = _attn_layer(x, xc, mod, i, nb, g1, p, need_ctx)
        else:
            p = (_pair_cols(conf_w_pw1[j], D // CONF_COLS, CONF_COLS).astype(BF16),
                 _pair_cols(row(conf_b_pw1[j]), D // CONF_COLS, CONF_COLS),
                 jnp.broadcast_to(conf_dw_w[j][:, None, :], (CONF_KERNEL, 8, D)), row(conf_dw_b[j]),
                 row(conf_ln_g[j]), row(conf_ln_b[j]), conf_w_pw2[j].astype(BF16), row(conf_b_pw2[j]))
            x = _conformer(x, mod, i, None, g1, p)
            if need_ctx:
                xc = _conformer(xc, mod, i, nb, g1, p)
        fp = _ffn_weights(ffn_w_up[i], ffn_conv_w[i], ffn_conv_b[i], ffn_w_down[i])
        g2 = row(norm2_g[i])
        x = _ffn(x, mod, i, None, g2, fp, final_g=row(final_g) if i == depth - 1 else None)
        if need_ctx:
            xc = _ffn(xc, mod, i, nb, g2, fp)
    return x
```

```python
import functools
import math

import jax
import jax.numpy as jnp
from jax import lax
from jax.experimental import pallas as pl
from jax.experimental.pallas import tpu as pltpu

F32 = jnp.float32
BF16 = jnp.bfloat16
HIGHEST = lax.Precision.HIGHEST

D = 1024
N_MOD = 6
N_MIXERS = 3
EPS = 1e-6
GRID_W = 64
ROPE_THETA = 10000.0

SSM_HEAD_DIM = 64
SSM_HEADS = 32
SSM_GROUPS = 8
SSM_HPG = SSM_HEADS // SSM_GROUPS
SSM_STATE = 128
SSM_CONV = 4
SSM_CHUNK = 128
D_INNER = SSM_HEADS * SSM_HEAD_DIM
SSM_GW = SSM_HPG * SSM_HEAD_DIM
SSM_BC = SSM_GROUPS * SSM_STATE
SSM_CONV_CH = D_INNER + 2 * SSM_BC
DT_PAD = 128
SSD_HALO = 16

DA_HEAD_DIM = 64
DA_HEADS = D // (2 * DA_HEAD_DIM)
DA_V = 2 * DA_HEAD_DIM

CONF_KERNEL = 31
CONF_HALO = 16
CONF_COLS = 256

FFN_HIDDEN = 2816
FFN_CHUNK = 256
FFN_NCHUNK = FFN_HIDDEN // FFN_CHUNK
FFN_HALO = 8
FFN_LOOKAHEAD = 3
FFN_DOWN_EVERY = 4

VMEM_LIMIT = 56 * 1024 * 1024
TOKEN_TILE = 512
RESID_ROW_BLOCKS = 2
ATTN_Q_TILE = 256
ATTN_HEADS_PER_STEP = 4
QKV_COLS = 512


def _sigmoid(x):
    return 1.0 / (1.0 + jnp.exp2(x * -math.log2(math.e)))


def _silu(x):
    return x * _sigmoid(x)


def _softplus(x):
    return jnp.maximum(x, 0.0) + jnp.log(1.0 + jnp.exp(-jnp.abs(x)))


def _rms(x):
    return x * lax.rsqrt(jnp.mean(x * x, axis=-1, keepdims=True) + EPS)


def _shift_tiles(x3, s, first, count):
    if s == 0:
        return x3[first:first + count]
    sub = lax.broadcasted_iota(jnp.int32, (count,) + x3.shape[1:], 1)
    rk = pltpu.roll(x3, (-s) % 8, 1)
    if s < 0:
        return jnp.where(sub >= -s, rk[first:first + count], rk[first - 1:first - 1 + count])
    return jnp.where(sub < 8 - s, rk[first:first + count], rk[first + 1:first + 1 + count])


def _dot01(m01, x, *, left):
    out, rest = None, x
    for _ in range(3):
        piece = rest.astype(BF16)
        part = jnp.dot(m01, piece, preferred_element_type=F32) if left else \
            jnp.dot(piece, m01, preferred_element_type=F32)
        out = part if out is None else out + part
        rest = rest - piece.astype(F32)
    return out


def _params(sem, vmem=VMEM_LIMIT):
    return pltpu.CompilerParams(dimension_semantics=sem, vmem_limit_bytes=vmem)


def _mod_spec(layer, row):
    if row is None:
        return pl.BlockSpec((None, None, N_MOD, D), lambda b, *_: (layer, b, 0, 0))
    return pl.BlockSpec((None, None, N_MOD, D), lambda b, *_: (layer, row, 0, 0))


def _const_spec(shape):
    nd = len(shape)
    return pl.BlockSpec(shape, lambda *_: (0,) * nd, pipeline_mode=pl.Buffered(1))


def _tile(n, pref):
    return pref if n % pref == 0 else n


def _mod_kernel(s_ref, w_ref, b_ref, o_ref):
    s = _silu(s_ref[...])
    o_ref[...] = jnp.dot(s, w_ref[...], preferred_element_type=F32, precision=HIGHEST) + b_ref[...]


def _mod_table(c, c_ctx, mod_w, mod_b):
    depth = mod_w.shape[0]
    nb = c.shape[0]
    rows = -(-(nb + 1) // 8) * 8
    s = jnp.zeros((rows, D), F32).at[:nb].set(c).at[nb].set(c_ctx)
    tn = 1536
    out = pl.pallas_call(
        _mod_kernel,
        grid=(depth, N_MOD * D // tn),
        in_specs=[pl.BlockSpec((rows, D), lambda i, n: (0, 0)),
                  pl.BlockSpec((None, D, tn), lambda i, n: (i, 0, n)),
                  pl.BlockSpec((None, 1, tn), lambda i, n: (i, 0, n))],
        out_specs=pl.BlockSpec((None, rows, tn), lambda i, n: (i, 0, n)),
        out_shape=jax.ShapeDtypeStruct((depth, rows, N_MOD * D), F32),
        compiler_params=_params(("arbitrary", "arbitrary")),
    )(s, mod_w, mod_b.reshape(depth, 1, N_MOD * D))
    return out.reshape(depth, rows, N_MOD, D)


def _mamba_in_kernel(x_ref, mod_ref, g_ref, wz_ref, wx_ref, wd_ref, z_ref, xbc_ref, dt_ref):
    gs = g_ref[...] * (1.0 + mod_ref[1:2, :])
    h = (_rms(x_ref[...]) * gs + mod_ref[0:1, :]).astype(BF16)
    for n in range(0, D_INNER, 1024):
        z_ref[:, n:n + 1024] = jnp.dot(h, wz_ref[:, n:n + 1024], preferred_element_type=F32).astype(z_ref.dtype)
    for n in range(0, SSM_CONV_CH, 1024):
        xbc_ref[:, n:n + 1024] = jnp.dot(h, wx_ref[:, n:n + 1024],
                                         preferred_element_type=F32).astype(xbc_ref.dtype)
    dt_ref[...] = jnp.dot(h, wd_ref[...], preferred_element_type=F32)


def _mamba_in(x, mod, layer, row, g, wz, wx, wd):
    nb, t, _ = x.shape
    tm = _tile(t, TOKEN_TILE)
    tok = lambda w: pl.BlockSpec((None, tm, w), lambda b, i: (b, i, 0))
    return pl.pallas_call(
        _mamba_in_kernel,
        grid=(nb, t // tm),
        in_specs=[tok(D), _mod_spec(layer, row), _const_spec((1, D)),
                  _const_spec((D, D_INNER)), _const_spec((D, SSM_CONV_CH)), _const_spec((D, DT_PAD))],
        out_specs=[tok(D_INNER), tok(SSM_CONV_CH), tok(DT_PAD)],
        out_shape=[jax.ShapeDtypeStruct((nb, t, D_INNER), BF16),
                   jax.ShapeDtypeStruct((nb, t, SSM_CONV_CH), BF16),
                   jax.ShapeDtypeStruct((nb, t, DT_PAD), F32)],
        compiler_params=_params(("parallel", "parallel")),
    )(x, mod, g, wz, wx, wd)


def _ssd_kernel(*refs, rev, nc, write_y, combine):
    it = iter(refs)
    u_ref, xh_ref, cw_ref, cbias_ref = (next(it) for _ in range(4))
    dt_ref, dtb_ref, alog_ref, dsk_ref, e_ref, h0_ref = (next(it) for _ in range(6))
    if combine:
        yf_ref = next(it)
    y_ref = next(it) if write_y else None
    hfin_ref = next(it)
    ht_ref = next(it)

    k = pl.program_id(1)
    ch = SSM_CHUNK
    off = SSM_HEADS if rev else 0

    @pl.when(k == 0)
    def _():
        ht_ref[...] = h0_ref[...]

    c = (nc - 1 - k) if rev else k
    side_on = (c < nc - 1) if rev else (c > 0)

    def load(c0, w):
        main = u_ref[:, c0:c0 + w].astype(F32)
        side = jnp.where(side_on, xh_ref[:, c0:c0 + w].astype(F32), 0.0)
        buf = jnp.concatenate([main, side] if rev else [side, main], axis=0)
        x3 = buf.reshape((ch + SSD_HALO) // 8, 8, w)
        first = 0 if rev else SSD_HALO // 8
        acc = cbias_ref[:, c0:c0 + w]
        for kk in range(SSM_CONV):
            s = (SSM_CONV - 1 - kk) * (1 if rev else -1)
            acc = acc + _shift_tiles(x3, s, first, ch // 8) * cw_ref[kk:kk + 1, c0:c0 + w]
        return _silu(acc).reshape(ch, w)

    dtv = _softplus(dt_ref[...] + dtb_ref[...])
    la = dtv * (-jnp.exp(alog_ref[...]))
    ri = lax.broadcasted_iota(jnp.int32, (ch, ch), 0)
    ci = lax.broadcasted_iota(jnp.int32, (ch, ch), 1)
    keep = (ci >= ri) if rev else (ci <= ri)
    causal = jnp.where(keep, 0.0, -jnp.inf)
    acum = _dot01(jnp.where(keep, 1.0, 0.0).astype(BF16), la, left=True)
    last = 0 if rev else ch - 1
    total = acum[last:last + 1, :]
    wcol = jnp.exp(total - acum) * dtv
    offv = jnp.exp(acum)
    acum2 = acum * math.log2(math.e)
    acum2_t = acum2.T
    dt_t = dtv.T

    lane = lax.broadcasted_iota(jnp.int32, (ch, SSM_GW), 1)

    wexp = jnp.dot(wcol.astype(BF16), e_ref[...], preferred_element_type=F32)
    sexp = jnp.dot(offv.astype(BF16), e_ref[...], preferred_element_type=F32)
    cdec = _dot01(e_ref[...], jnp.broadcast_to(offv[last:last + 1, :], (8, DT_PAD)), left=False)[0:1, :]

    for g in range(SSM_GROUPS):
        gc = slice(g * SSM_GW, (g + 1) * SSM_GW)
        xg = load(g * SSM_GW, SSM_GW)
        xg16 = xg.astype(BF16)
        bg16 = load(D_INNER + g * SSM_STATE, SSM_STATE).astype(BF16)
        xw = (xg * wexp[:, gc]).astype(BF16)
        st_t = lax.dot_general(bg16, xw, (((0,), (0,)), ((), ())), preferred_element_type=F32)
        sg = sexp[:, gc]
        ht_old = ht_ref[g]
        if write_y:
            cg16 = load(D_INNER + SSM_BC + g * SSM_STATE, SSM_STATE).astype(BF16)
            cbm = lax.dot_general(cg16, bg16, (((1,), (1,)), ((), ())), preferred_element_type=F32)
            yoff = jnp.dot(cg16, ht_old.astype(BF16), preferred_element_type=F32)
            ms, xs = [], []
            for j in range(SSM_HPG):
                col = off + SSM_HPG * g + j
                seg = acum2[:, col:col + 1] - acum2_t[col:col + 1, :]
                lmat = jnp.exp2(seg + causal)
                ms.append((cbm * lmat * dt_t[col:col + 1, :]).astype(BF16))
                in_head = (lane >= j * SSM_HEAD_DIM) & (lane < (j + 1) * SSM_HEAD_DIM)
                xs.append(jnp.where(in_head, xg16, jnp.zeros_like(xg16)))
            ydiag = jnp.dot(jnp.concatenate(ms, axis=1), jnp.concatenate(xs, axis=0),
                            preferred_element_type=F32)
            y = ydiag + yoff * sg + dsk_ref[:, g * SSM_GW:(g + 1) * SSM_GW] * xg
            cols = slice(g * SSM_GW, (g + 1) * SSM_GW)
            if combine:
                y = y + yf_ref[:, cols].astype(F32)
            y_ref[:, cols] = y.astype(y_ref.dtype)
        ht_ref[g] = ht_old * cdec[:, gc] + st_t

    @pl.when(k == nc - 1)
    def _():
        hfin_ref[...] = ht_ref[...]


def _ssd(u, dt, h0, dt_bias, a_log, d_skip, taps, *, rev, write_y, combine=None, y_dtype=F32):
    nb, t, _ = u.shape
    nc = t // SSM_CHUNK
    hb = SSM_CHUNK // SSD_HALO
    if rev:
        cidx = lambda k: nc - 1 - k
        hidx = lambda k: jnp.minimum((nc - k) * hb, t // SSD_HALO - 1)
    else:
        cidx = lambda k: k
        hidx = lambda k: jnp.maximum(k * hb - 1, 0)
    chunk = lambda w: pl.BlockSpec((None, SSM_CHUNK, w), lambda b, k: (b, cidx(k), 0))
    state = pl.BlockSpec((None, SSM_GROUPS, SSM_STATE, SSM_GW), lambda b, k: (b, 0, 0, 0))
    in_specs = [chunk(SSM_CONV_CH), pl.BlockSpec((None, SSD_HALO, SSM_CONV_CH), lambda b, k: (b, hidx(k), 0)),
                _const_spec((SSM_CONV, SSM_CONV_CH)), _const_spec((1, SSM_CONV_CH))]
    args = [u, u, taps[0], taps[1]]
    scratch = [pltpu.VMEM((SSM_GROUPS, SSM_STATE, SSM_GW), F32)]
    off = SSM_HEADS if rev else 0
    spread = jnp.repeat(jnp.eye(DT_PAD, dtype=BF16)[:, off:off + SSM_HEADS], SSM_HEAD_DIM, axis=1)
    in_specs += [chunk(DT_PAD), _const_spec((1, DT_PAD)), _const_spec((1, DT_PAD)), _const_spec((1, D_INNER)),
                 _const_spec((DT_PAD, D_INNER)), state]
    args += [dt, dt_bias, a_log, d_skip, spread, h0]
    out_specs, out_shape = [], []
    if combine is not None:
        in_specs.append(chunk(D_INNER))
        args.append(combine)
    if write_y:
        out_specs.append(chunk(D_INNER))
        out_shape.append(jax.ShapeDtypeStruct((nb, t, D_INNER), y_dtype))
    out_specs.append(state)
    out_shape.append(jax.ShapeDtypeStruct((nb, SSM_GROUPS, SSM_STATE, SSM_GW), F32))
    outs = pl.pallas_call(
        functools.partial(_ssd_kernel, rev=rev, nc=nc, write_y=write_y, combine=combine is not None),
        grid=(nb, nc),
        in_specs=in_specs, out_specs=out_specs, out_shape=out_shape,
        scratch_shapes=scratch,
        compiler_params=_params(("parallel", "arbitrary")),
    )(*args)
    return outs if write_y else (None, outs[0])


def _resid_kernel(*refs, gate_row, ssm_gate):
    if not ssm_gate:
        a_ref, w_ref, b_ref, x_ref, mod_ref, o_ref = refs
        y = jnp.dot(a_ref[...], w_ref[...], preferred_element_type=F32) + b_ref[...]
        o_ref[...] = x_ref[...] + mod_ref[gate_row:gate_row + 1, :] * y
        return
    a_ref, z_ref, ng_ref, w_ref, b_ref, x_ref, mod_ref, o_ref = refs
    tm = a_ref.shape[0]
    rb = tm // RESID_ROW_BLOCKS
    for r0 in range(0, tm, rb):
        parts = []
        for g in range(SSM_GROUPS):
            cols = slice(g * SSM_GW, (g + 1) * SSM_GW)
            yg = a_ref[r0:r0 + rb, cols].astype(F32) * _silu(z_ref[r0:r0 + rb, cols].astype(F32))
            parts.append((_rms(yg) * ng_ref[:, cols]).astype(BF16))
        y = jnp.dot(jnp.concatenate(parts, axis=1), w_ref[...], preferred_element_type=F32) + b_ref[...]
        o_ref[r0:r0 + rb, :] = x_ref[r0:r0 + rb, :] + mod_ref[gate_row:gate_row + 1, :] * y


def _resid_matmul(a, w, bias, x, mod, layer, row, gate_row, ssm_gate=None):
    nb, t, kdim = a.shape
    tm = _tile(t, TOKEN_TILE)
    tok = lambda wd: pl.BlockSpec((None, tm, wd), lambda b, i: (b, i, 0))
    in_specs, args = [tok(kdim)], [a]
    if ssm_gate is not None:
        in_specs += [tok(kdim), _const_spec((1, kdim))]
        args += list(ssm_gate)
    in_specs += [_const_spec((kdim, D)), _const_spec((1, D)), tok(D), _mod_spec(layer, row)]
    args += [w, bias, x, mod]
    return pl.pallas_call(
        functools.partial(_resid_kernel, gate_row=gate_row, ssm_gate=ssm_gate is not None),
        grid=(nb, t // tm),
        in_specs=in_specs,
        out_specs=tok(D),
        out_shape=jax.ShapeDtypeStruct((nb, t, D), F32),
        compiler_params=_params(("parallel", "parallel")),
    )(*args)


def _mamba_layer(x, xc, mod, layer, nb, g1, p, need_ctx):
    wz, wx, wd, cw, cbias, dtb, alog, dsk, ng, wout = p
    z_l, xbc_l, dt_l = _mamba_in(x, mod, layer, None, g1, wz, wx, wd)
    z_c, xbc_c, dt_c = _mamba_in(xc, mod, layer, nb, g1, wz, wx, wd)
    h0 = jnp.zeros((x.shape[0], SSM_GROUPS, SSM_STATE, SSM_GW), F32)
    dirp = lambda d: (dtb, alog, dsk[d])
    taps = lambda d: (cw[d], cbias[d])
    yf_c, hf = _ssd(xbc_c, dt_c, h0, *dirp(0), rev=False, write_y=need_ctx, taps=taps(0))
    yf_l, _ = _ssd(xbc_l, dt_l, hf, *dirp(0), rev=False, write_y=True, taps=taps(0))
    y_c, hb = _ssd(xbc_c, dt_c, h0, *dirp(1), rev=True, write_y=need_ctx, taps=taps(1),
                   combine=yf_c if need_ctx else None, y_dtype=BF16)
    y_l, _ = _ssd(xbc_l, dt_l, hb, *dirp(1), rev=True, write_y=True, taps=taps(1), combine=yf_l, y_dtype=BF16)
    zero_b = jnp.zeros((1, D), F32)
    x = _resid_matmul(y_l, wout, zero_b, x, mod, layer, None, 2, ssm_gate=(z_l, ng))
    if need_ctx:
        xc = _resid_matmul(y_c, wout, zero_b, xc, mod, layer, nb, 2, ssm_gate=(z_c, ng))
    return x, xc


def _qkv_kernel(*refs, rope):
    if rope:
        x_ref, mod_ref, g_ref, wq_ref, wk_ref, wv_ref, cos_ref, sin_ref, q_ref, k_ref, v_ref = refs
    else:
        x_ref, mod_ref, g_ref, wq_ref, wk_ref, wv_ref, q_ref, k_ref, v_ref = refs
    gs = g_ref[...] * (1.0 + mod_ref[1:2, :])
    h = (_rms(x_ref[...]) * gs + mod_ref[0:1, :]).astype(BF16)
    tm = h.shape[0]
    if rope:
        cos, sin = cos_ref[...], sin_ref[...]
        lane = lax.broadcasted_iota(jnp.int32, (tm, 128), 1)
        first = (lane % 32) < 16

    def rot(a_ref, w_ref, scale):
        for j2 in range(D // QKV_COLS):
            wide = jnp.dot(h, w_ref[:, j2 * QKV_COLS:(j2 + 1) * QKV_COLS], preferred_element_type=F32)
            for j in range(QKV_COLS // 128):
                blk = wide[:, j * 128:(j + 1) * 128]
                if scale != 1.0:
                    blk = blk * scale
                if rope:
                    partner = jnp.where(first, pltpu.roll(blk, 112, 1), pltpu.roll(blk, 16, 1))
                    blk = blk * cos + partner * sin
                c0 = j2 * QKV_COLS + j * 128
                a_ref[:, c0:c0 + 128] = blk.astype(a_ref.dtype)

    rot(q_ref, wq_ref, DA_HEAD_DIM ** -0.5 * math.log2(math.e))
    rot(k_ref, wk_ref, 1.0)
    v_ref[...] = jnp.dot(h, wv_ref[...], preferred_element_type=F32).astype(v_ref.dtype)


def _rope_tables(s):
    lane = jnp.arange(128)
    pos = jnp.arange(s)
    coord = jnp.where(((lane % 64) // 32 == 0)[None, :], (pos // GRID_W)[:, None], (pos % GRID_W)[:, None])
    n_freq = DA_HEAD_DIM // 4
    inv = ROPE_THETA ** (-(lane % n_freq).astype(F32) / n_freq)
    ang = coord.astype(F32) * inv[None, :]
    sign = jnp.where((lane % 32) < 16, -1.0, 1.0)
    return jnp.cos(ang), jnp.sin(ang) * sign[None, :]


def _qkv(x, mod, layer, row, g, wq, wk, wv, tables):
    nb, t, _ = x.shape
    tm = _tile(t, TOKEN_TILE)
    tok = lambda: pl.BlockSpec((None, tm, D), lambda b, i: (b, i, 0))
    in_specs = [tok(), _mod_spec(layer, row), _const_spec((1, D))] + [_const_spec((D, D))] * 3
    args = [x, mod, g, wq, wk, wv]
    if tables is not None:
        in_specs += [pl.BlockSpec((tm, 128), lambda b, i: (i, 0))] * 2
        args += list(tables)
    return pl.pallas_call(
        functools.partial(_qkv_kernel, rope=tables is not None),
        grid=(nb, t // tm),
        in_specs=in_specs, out_specs=[tok()] * 3,
        out_shape=[jax.ShapeDtypeStruct((nb, t, D), BF16)] * 3,
        compiler_params=_params(("parallel", "parallel")),
    )(*args)


def _attn_kernel(*refs, nseg, lam_init):
    lam_ref, ng_ref, q_ref = refs[:3]
    kv = refs[3:3 + 2 * nseg]
    o_ref = refs[3 + 2 * nseg]
    tq = q_ref.shape[0]
    nh = q_ref.shape[1] // DA_V
    lane = lax.broadcasted_iota(jnp.int32, (tq, DA_V), 1)
    lp = lam_ref[...]
    lam = (jnp.exp(jnp.sum(lp[0:1] * lp[1:2], axis=-1, keepdims=True))
           - jnp.exp(jnp.sum(lp[2:3] * lp[3:4], axis=-1, keepdims=True)) + lam_init)

    def head_scores(hh):
        hc = slice(hh * DA_V, (hh + 1) * DA_V)
        q = q_ref[:, hc]
        zero = jnp.zeros_like(q)
        qs = [jnp.where((lane >= e * DA_HEAD_DIM) & (lane < (e + 1) * DA_HEAD_DIM), q, zero) for e in range(2)]
        return [[lax.dot_general(qe, kv[2 * i][:, hc], (((1,), (1,)), ((), ())), preferred_element_type=F32)
                 for i in range(nseg)] for qe in qs]

    def head_out(hh, scores):
        hc = slice(hh * DA_V, (hh + 1) * DA_V)
        vs = [jnp.concatenate([kv[2 * i + 1][:, hc], jnp.ones((kv[2 * i + 1].shape[0], DA_V), BF16)], axis=1)
              for i in range(nseg)]
        halves = []
        for e in range(2):
            m = scores[e][0].max(axis=-1, keepdims=True)
            for s in scores[e][1:]:
                m = jnp.maximum(m, s.max(axis=-1, keepdims=True))
            acc = jnp.zeros((tq, 2 * DA_V), F32)
            for i in range(nseg):
                p = jnp.exp2(scores[e][i] - m)
                acc = acc + jnp.dot(p.astype(BF16), vs[i], preferred_element_type=F32)
            halves.append(acc[:, :DA_V] / acc[:, DA_V:DA_V + 1])
        o = halves[0] - lam * halves[1]
        o_ref[:, hc] = (_rms(o) * ng_ref[...] * (1.0 - lam_init)).astype(o_ref.dtype)

    all_scores = [head_scores(hh) for hh in range(nh)]
    for hh in range(nh):
        head_out(hh, all_scores[hh])


def _attention(q, segs, lam_p, ng, lam_init):
    nb, t, _ = q.shape
    tq = _tile(t, ATTN_Q_TILE)
    hw = ATTN_HEADS_PER_STEP * DA_V
    in_specs = [_const_spec((4, DA_HEAD_DIM)), _const_spec((1, DA_V)),
                pl.BlockSpec((None, tq, hw), lambda b, h, i: (b, i, h))]
    args = [lam_p, ng, q]
    for k, v in segs:
        tk = k.shape[1]
        in_specs += [pl.BlockSpec((None, tk, hw), lambda b, h, i: (b, 0, h))] * 2
        args += [k, v]
    return pl.pallas_call(
        functools.partial(_attn_kernel, nseg=len(segs), lam_init=lam_init),
        grid=(nb, DA_HEADS // ATTN_HEADS_PER_STEP, t // tq),
        in_specs=in_specs,
        out_specs=pl.BlockSpec((None, tq, hw), lambda b, h, i: (b, i, h)),
        out_shape=jax.ShapeDtypeStruct((nb, t, D), BF16),
        compiler_params=_params(("parallel", "parallel", "arbitrary")),
    )(*args)


def _attn_layer(x, xc, mod, layer, nb, g1, p, need_ctx):
    wq, wk, wv, lam_p, ng, wout = p
    lam_init = 0.8 - 0.6 * math.exp(-0.3 * layer)
    q_l, k_l, v_l = _qkv(x, mod, layer, None, g1, wq, wk, wv, _rope_tables(x.shape[1]))
    q_c, k_c, v_c = _qkv(xc, mod, layer, nb, g1, wq, wk, wv, None)
    zero_b = jnp.zeros((1, D), F32)
    o_l = _attention(q_l, [(k_c, v_c), (k_l, v_l)], lam_p, ng, lam_init)
    x = _resid_matmul(o_l, wout, zero_b, x, mod, layer, None, 2)
    if need_ctx:
        o_c = _attention(q_c, [(k_c, v_c)], lam_p, ng, lam_init)
        xc = _resid_matmul(o_c, wout, zero_b, xc, mod, layer, nb, 2)
    return x, xc


def _conf_kernel(xm_ref, xp_ref, xn_ref, mod_ref, g_ref, w1_ref, b1_ref, dw_ref, db_ref, lg_ref, lb_ref,
                 w2_ref, b2_ref, o_ref, h_ref, u_ref, c_ref, sh_ref, *, tm, nt):
    t = pl.program_id(1)
    hl = CONF_HALO
    gs = g_ref[...] * (1.0 + mod_ref[1:2, :])
    shift = mod_ref[0:1, :]
    mod = lambda x: _rms(x) * gs + shift
    h_ref[...] = jnp.concatenate([mod(xp_ref[...]), mod(xm_ref[...]), mod(xn_ref[...])], axis=0).astype(BF16)
    rowid = lax.broadcasted_iota(jnp.int32, (tm + 2 * hl, 1), 0)
    valid = ((rowid >= hl) | (t > 0)) & ((rowid < hl + tm) | (t < nt - 1))

    rb, cw = 64, CONF_COLS
    pad = CONF_KERNEL // 2
    nt8 = (tm + 2 * hl) // 8
    glu_in = lambda cb: jnp.dot(h_ref[...], w1_ref[cb], preferred_element_type=F32) + b1_ref[cb]
    a_next = glu_in(0)
    for c0 in range(0, D, cw):
        a = a_next
        if c0 + cw < D:
            a_next = glu_in(c0 // cw + 1)
        u = jnp.where(valid, a[:, :cw] * _sigmoid(a[:, cw:]), 0.0)
        u_ref[:, c0:c0 + cw] = u
        u3 = u.reshape(nt8, 8, cw)
        for r in range(1, 8):
            sh_ref[r - 1, 8:, c0:c0 + cw] = _shift_tiles(u3, -r, 1, nt8 - 1).reshape(8 * (nt8 - 1), cw)
        for r0 in range(0, tm, rb):
            acc = jnp.broadcast_to(db_ref[:, c0:c0 + cw], (rb // 8, 8, cw))
            for kk in range(CONF_KERNEL):
                d = kk - pad
                r = (-d) % 8
                s = hl + d + r + r0
                src = u_ref if r == 0 else sh_ref.at[r - 1]
                acc = acc + src[s:s + rb, c0:c0 + cw].reshape(rb // 8, 8, cw) * dw_ref[kk, :, c0:c0 + cw]
            c_ref[r0:r0 + rb, c0:c0 + cw] = acc.reshape(rb, cw)

    cv = c_ref[...]
    mu = jnp.mean(cv, axis=-1, keepdims=True)
    xc = cv - mu
    ln = xc * lax.rsqrt(jnp.mean(xc * xc, axis=-1, keepdims=True) + EPS) * lg_ref[...] + lb_ref[...]
    y = jnp.dot(_silu(ln).astype(BF16), w2_ref[...], preferred_element_type=F32) + b2_ref[...]
    o_ref[...] = xm_ref[...] + mod_ref[2:3, :] * y


def _conformer(x, mod, layer, row, g, p):
    w1, b1, dw, db, lg, lb, w2, b2 = p
    nb, t, _ = x.shape
    tm = _tile(t, 512)
    nt = t // tm
    hl = CONF_HALO
    r = tm // hl
    return pl.pallas_call(
        functools.partial(_conf_kernel, tm=tm, nt=nt),
        grid=(nb, nt),
        in_specs=[pl.BlockSpec((None, tm, D), lambda b, i: (b, i, 0)),
                  pl.BlockSpec((None, hl, D), lambda b, i: (b, jnp.maximum(i * r - 1, 0), 0)),
                  pl.BlockSpec((None, hl, D), lambda b, i: (b, jnp.minimum((i + 1) * r, t // hl - 1), 0)),
                  _mod_spec(layer, row), _const_spec((1, D)),
                  _const_spec((D // CONF_COLS, D, 2 * CONF_COLS)), _const_spec((D // CONF_COLS, 1, 2 * CONF_COLS)),
                  _const_spec((CONF_KERNEL, 8, D)), _const_spec((1, D)), _const_spec((1, D)), _const_spec((1, D)),
                  _const_spec((D, D)), _const_spec((1, D))],
        out_specs=pl.BlockSpec((None, tm, D), lambda b, i: (b, i, 0)),
        out_shape=jax.ShapeDtypeStruct((nb, t, D), F32),
        scratch_shapes=[pltpu.VMEM((tm + 2 * hl, D), BF16), pltpu.VMEM((tm + 2 * hl, D), F32),
                        pltpu.VMEM((tm, D), F32), pltpu.VMEM((7, tm + 2 * hl, D), F32)],
        compiler_params=_params(("parallel", "parallel")),
    )(x, x, x, mod, g, w1, b1, dw, db, lg, lb, w2, b2)


def _ffn_kernel(*refs, tm, nt, final):
    if final:
        (xm_ref, xp_ref, xn_ref, mod_ref, g_ref, wu_ref, cw_ref, cb_ref, wd_ref, fg_ref,
         o_ref, h_ref, act_ref) = refs
    else:
        (xm_ref, xp_ref, xn_ref, mod_ref, g_ref, wu_ref, cw_ref, cb_ref, wd_ref,
         o_ref, h_ref, act_ref) = refs
    t = pl.program_id(1)
    hl = FFN_HALO
    gs = g_ref[...] * (1.0 + mod_ref[4:5, :])
    shift = mod_ref[3:4, :]
    mod = lambda x: _rms(x) * gs + shift
    hp = jnp.where(t > 0, mod(xp_ref[...]), 0.0)
    hn = jnp.where(t < nt - 1, mod(xn_ref[...]), 0.0)
    h_ref[...] = jnp.concatenate([hp, mod(xm_ref[...]), hn], axis=0).astype(BF16)

    nt8, nm8 = (tm + 2 * hl) // 8, tm // 8
    y, k0 = None, 0
    up = lambda j: jnp.dot(h_ref[...], wu_ref[j], preferred_element_type=F32).reshape(nt8, 8, 2 * FFN_CHUNK)
    ahead = [up(j) for j in range(FFN_LOOKAHEAD)]
    for j in range(FFN_NCHUNK):
        u3 = ahead.pop(0)
        if j + FFN_LOOKAHEAD < FFN_NCHUNK:
            ahead.append(up(j + FFN_LOOKAHEAD))
        w = cw_ref[j]
        cv = cb_ref[j]
        for kk in range(3):
            cv = cv + _shift_tiles(u3, kk - 1, 1, nm8) * w[kk:kk + 1, :]
        cv = cv.reshape(tm, 2 * FFN_CHUNK)
        act_ref[:, j * FFN_CHUNK:(j + 1) * FFN_CHUNK] = (_silu(cv[:, :FFN_CHUNK]) * cv[:, FFN_CHUNK:]).astype(BF16)
        if (j + 1) % FFN_DOWN_EVERY == 0 or j == FFN_NCHUNK - 1:
            k1 = (j + 1) * FFN_CHUNK
            part = jnp.dot(act_ref[:, k0:k1], wd_ref[k0:k1, :], preferred_element_type=F32)
            y = part if y is None else y + part
            k0 = k1
    out = xm_ref[...] + mod_ref[5:6, :] * y
    if final:
        out = _rms(out) * fg_ref[...]
    o_ref[...] = out


def _ffn(x, mod, layer, row, g, p, final_g=None):
    wu, cw, cb, wd = p
    nb, t, _ = x.shape
    tm = _tile(t, TOKEN_TILE)
    nt = t // tm
    hl = FFN_HALO
    r = tm // hl
    final = final_g is not None
    in_specs = [pl.BlockSpec((None, tm, D), lambda b, i: (b, i, 0)),
                pl.BlockSpec((None, hl, D), lambda b, i: (b, jnp.maximum(i * r - 1, 0), 0)),
                pl.BlockSpec((None, hl, D), lambda b, i: (b, jnp.minimum((i + 1) * r, t // hl - 1), 0)),
                _mod_spec(layer, row), _const_spec((1, D)),
                _const_spec((FFN_NCHUNK, D, 2 * FFN_CHUNK)), _const_spec((FFN_NCHUNK, 3, 2 * FFN_CHUNK)),
                _const_spec((FFN_NCHUNK, 1, 2 * FFN_CHUNK)), _const_spec((FFN_HIDDEN, D))]
    args = [x, x, x, mod, g, wu, cw, cb, wd]
    if final:
        in_specs.append(_const_spec((1, D)))
        args.append(final_g)
    return pl.pallas_call(
        functools.partial(_ffn_kernel, tm=tm, nt=nt, final=final),
        grid=(nb, nt),
        in_specs=in_specs,
        out_specs=pl.BlockSpec((None, tm, D), lambda b, i: (b, i, 0)),
        out_shape=jax.ShapeDtypeStruct((nb, t, D), F32),
        scratch_shapes=[pltpu.VMEM((tm + 2 * hl, D), BF16), pltpu.VMEM((tm, FFN_HIDDEN), BF16)],
        compiler_params=_params(("parallel", "parallel")),
    )(*args)


def _pair_cols(a, nchunk, width):
    lead = a.shape[:-1]
    a = a.reshape(lead + (2, nchunk, width))
    a = jnp.moveaxis(a, -2, 0)
    return a.reshape((nchunk,) + lead + (2 * width,))


def _ffn_weights(w_up, conv_w, conv_b, w_down):
    pair = lambda a: _pair_cols(a, FFN_NCHUNK, FFN_CHUNK)
    return (pair(w_up).astype(BF16), pair(conv_w), pair(conv_b[None, :]),
            w_down.astype(BF16))


def kernel(x, c, ctx, c_ctx, mod_w, mod_b, norm1_g, norm2_g, ffn_w_up, ffn_conv_w, ffn_conv_b, ffn_w_down,
           ssm_w_in, ssm_conv_w, ssm_conv_b, ssm_dt_bias, ssm_a_log, ssm_d, ssm_norm_g, ssm_w_out, attn_w_in,
           attn_lambda, attn_norm_g, attn_w_out, conf_w_pw1, conf_b_pw1, conf_dw_w, conf_dw_b, conf_ln_g,
           conf_ln_b, conf_w_pw2, conf_b_pw2, final_g):
    depth = mod_w.shape[0]
    nb = x.shape[0]
    mod = _mod_table(c, c_ctx, mod_w, mod_b)
    xc = ctx
    row = lambda a: a.reshape(1, -1)
    pad_dt = lambda a: jnp.pad(a.reshape(1, -1), ((0, 0), (0, DT_PAD - 2 * SSM_HEADS)))
    for i in range(depth):
        kind, j = i % N_MIXERS, i // N_MIXERS
        need_ctx = i < depth - 1
        g1 = row(norm1_g[i])
        if kind == 0:
            w_in = ssm_w_in[j]
            o_x, o_dt = D_INNER, D_INNER + SSM_CONV_CH
            wd = jnp.pad(w_in[:, o_dt:], ((0, 0), (0, DT_PAD - 2 * SSM_HEADS)))
            p = (w_in[:, :o_x].astype(BF16), w_in[:, o_x:o_dt].astype(BF16), wd.astype(BF16),
                 ssm_conv_w[j], ssm_conv_b[j][:, None, :], pad_dt(ssm_dt_bias[j]), pad_dt(ssm_a_log[j]),
                 jnp.repeat(ssm_d[j], SSM_HEAD_DIM, axis=-1)[:, None, :], row(ssm_norm_g[j]),
                 ssm_w_out[j].astype(BF16))
            x, xc = _mamba_layer(x, xc, mod, i, nb, g1, p, need_ctx)
        elif kind == 1:
            w_in = attn_w_in[j].astype(BF16)
            p = (w_in[:, :D], w_in[:, D:2 * D], w_in[:, 2 * D:], attn_lambda[j], row(attn_norm_g[j]),
                 attn_w_out[j].astype(BF16))
            x, xc = _attn_layer(x, xc, mod, i, nb, g1, p, need_ctx)
        else:
            p = (_pair_cols(conf_w_pw1[j], D // CONF_COLS, CONF_COLS).astype(BF16),
                 _pair_cols(row(conf_b_pw1[j]), D // CONF_COLS, CONF_COLS),
                 jnp.broadcast_to(conf_dw_w[j][:, None, :], (CONF_KERNEL, 8, D)), row(conf_dw_b[j]),
                 row(conf_ln_g[j]), row(conf_ln_b[j]), conf_w_pw2[j].astype(BF16), row(conf_b_pw2[j]))
            x = _conformer(x, mod, i, None, g1, p)
            if need_ctx:
                xc = _conformer(xc, mod, i, nb, g1, p)
        fp = _ffn_weights(ffn_w_up[i], ffn_conv_w[i], ffn_conv_b[i], ffn_w_down[i])
        g2 = row(norm2_g[i])
        x = _ffn(x, mod, i, None, g2, fp, final_g=row(final_g) if i == depth - 1 else None)
        if need_ctx:
            xc = _ffn(xc, mod, i, nb, g2, fp)
    return x
```
